```python
import jax, jax.numpy as jnp
from jax import lax
import numpy as np

D_MODEL = 1024
BATCH = 8
SEQ = 8192
DEPTH = 1

D_PLE = 256
D_MIX = 2 * D_MODEL
GM_WIDTH = D_MIX // 2
GM_HEADS = 8
GM_HEAD_DIM = GM_WIDTH // GM_HEADS
GM_CHUNK = 128
SSM_WIDTH = D_MIX - GM_WIDTH
SSM_HEAD_DIM = 64
SSM_HEADS = SSM_WIDTH // SSM_HEAD_DIM
SSM_GROUPS = 2
SSM_STATE = 128
SSM_CONV = 4
SSM_CHUNK = 128
SSM_CONV_DIM = SSM_WIDTH + 2 * SSM_GROUPS * SSM_STATE
D_FF = 256 * ((8 * D_MODEL // 3 + 255) // 256)
EPS = 1e-6
IN_SPLITS = (GM_WIDTH, 2 * GM_WIDTH, 2 * GM_WIDTH + SSM_WIDTH, 2 * GM_WIDTH + SSM_WIDTH + SSM_CONV_DIM)
IN_PROJ_DIM = 2 * GM_WIDTH + SSM_WIDTH + SSM_CONV_DIM + SSM_HEADS

kernel_name = "hybrid_gmlp_ssd_macaron_block"


def rmsnorm(x, g):
    xf = x.astype(jnp.float32)
    y = xf * lax.rsqrt(jnp.mean(xf * xf, axis=-1, keepdims=True) + EPS)
    return (y * g.astype(jnp.float32)).astype(x.dtype)


def layernorm(x, g, b):
    xf = x.astype(jnp.float32)
    mu = jnp.mean(xf, axis=-1, keepdims=True)
    xc = xf - mu
    y = xc * lax.rsqrt(jnp.mean(xc * xc, axis=-1, keepdims=True) + EPS)
    return (y * g.astype(jnp.float32) + b.astype(jnp.float32)).astype(x.dtype)


def swiglu(x, w_gate, w_up, w_down):
    return (jax.nn.silu(x @ w_gate) * (x @ w_up)) @ w_down


def chunked_spatial_gating(u, v, ln_g, ln_b, w_s, b_s):
    bsz, L, _ = u.shape
    nc = L // GM_CHUNK
    v = layernorm(v, ln_g, ln_b).reshape(bsz, nc, GM_CHUNK, GM_HEADS, GM_HEAD_DIM)
    mask = jnp.tril(jnp.ones((GM_CHUNK, GM_CHUNK), dtype=bool))
    w = jnp.where(mask, w_s, jnp.zeros_like(w_s)).astype(v.dtype)
    mixed = jnp.einsum("hts,bcshd->bcthd", w, v) + b_s.T.astype(v.dtype)[None, None, :, :, None]
    return u * mixed.reshape(bsz, L, GM_WIDTH)


def causal_depthwise_conv(x, w, b):
    y = lax.conv_general_dilated(
        x, w[:, None, :].astype(x.dtype), window_strides=(1,), padding=[(SSM_CONV - 1, 0)],
        dimension_numbers=("NWC", "WIO", "NWC"), feature_group_count=x.shape[-1])
    return y + b.astype(x.dtype)


def segsum_exp(cs):
    T = cs.shape[-1]
    diff = cs[..., :, None] - cs[..., None, :]
    mask = jnp.tril(jnp.ones((T, T), dtype=bool))
    return jnp.exp(jnp.where(mask, diff, -jnp.inf))


def ssd_chunked(x, dt, a, bm, cm):
    bsz, L, H, P = x.shape
    nc = L // SSM_CHUNK
    k = H // SSM_GROUPS
    xdt = (x * dt[..., None]).reshape(bsz, nc, SSM_CHUNK, SSM_GROUPS, k, P)
    adt = (dt * a).reshape(bsz, nc, SSM_CHUNK, SSM_GROUPS, k).transpose(0, 3, 4, 1, 2)
    bm = bm.reshape(bsz, nc, SSM_CHUNK, SSM_GROUPS, SSM_STATE)
    cm = cm.reshape(bsz, nc, SSM_CHUNK, SSM_GROUPS, SSM_STATE)
    a_cs = jnp.cumsum(adt, axis=-1)
    decay = segsum_exp(a_cs)
    cb = jnp.einsum("bclgn,bcsgn->bgcls", cm, bm)
    y_diag = jnp.einsum("bgkcls,bcsgkp->bclgkp", cb[:, :, None] * decay, xdt)
    decay_states = jnp.exp(a_cs[..., -1:] - a_cs).transpose(0, 3, 4, 1, 2)
    states = jnp.einsum("bclgn,bclgkp->bcgkpn", bm, xdt * decay_states[..., None])
    chunk_tot = jnp.pad(a_cs[..., -1], ((0, 0), (0, 0), (0, 0), (1, 0)))
    decay_chunk = segsum_exp(jnp.cumsum(chunk_tot, axis=-1))
    states = jnp.concatenate([jnp.zeros_like(states[:, :1]), states], axis=1)
    new_states = jnp.einsum("bgkzc,bcgkpn->bzgkpn", decay_chunk, states)
    prev_states = new_states[:, :-1]
    out_decay = jnp.exp(a_cs).transpose(0, 3, 4, 1, 2)
    y_off = jnp.einsum("bclgn,bcgkpn->bclgkp", cm, prev_states) * out_decay[..., None]
    return (y_diag + y_off).reshape(bsz, L, H, P)


def mamba2_mixer(z, xbc, dt_raw, conv_w, conv_b, dt_bias, a_log, d_skip, norm_g):
    bsz, L, _ = z.shape
    f32 = jnp.float32
    xbc = jax.nn.silu(causal_depthwise_conv(xbc, conv_w, conv_b))
    xs, bm, cm = jnp.split(xbc, [SSM_WIDTH, SSM_WIDTH + SSM_GROUPS * SSM_STATE], axis=-1)
    xs = xs.reshape(bsz, L, SSM_HEADS, SSM_HEAD_DIM).astype(f32)
    bm = bm.reshape(bsz, L, SSM_GROUPS, SSM_STATE).astype(f32)
    cm = cm.reshape(bsz, L, SSM_GROUPS, SSM_STATE).astype(f32)
    dt = jax.nn.softplus(dt_raw.astype(f32) + dt_bias.astype(f32))
    a = -jnp.exp(a_log.astype(f32))
    y = ssd_chunked(xs, dt, a, bm, cm) + xs * d_skip.astype(f32)[:, None]
    y = y.reshape(bsz, L, SSM_WIDTH) * jax.nn.silu(z.astype(f32))
    y = y.reshape(bsz, L, SSM_GROUPS, SSM_WIDTH // SSM_GROUPS)
    y = y * lax.rsqrt(jnp.mean(y * y, axis=-1, keepdims=True) + EPS)
    return (y.reshape(bsz, L, SSM_WIDTH) * norm_g.astype(f32)).astype(z.dtype)


def setup_inputs(seed: int = 0) -> dict:
    key = jax.random.key(seed)
    ks = iter(jax.random.split(key, 40))

    def nrm(shape, scale):
        return jax.random.normal(next(ks), shape, jnp.float32) * scale

    def gain(shape):
        return 1.0 + 0.1 * jax.random.normal(next(ks), shape, jnp.float32)

    L = DEPTH
    x = jax.random.normal(next(ks), (BATCH, SEQ, D_MODEL), jnp.float32)
    p = jax.random.normal(next(ks), (DEPTH, BATCH, SEQ, D_PLE), jnp.float32)
    dt0 = jnp.exp(jax.random.uniform(next(ks), (L, SSM_HEADS), jnp.float32)
                  * (np.log(0.1) - np.log(0.001)) + np.log(0.001))
    dt0 = jnp.maximum(dt0, 1e-4)
    dt_bias = dt0 + jnp.log(-jnp.expm1(-dt0))
    a_log = jnp.log(jax.random.uniform(next(ks), (L, SSM_HEADS), jnp.float32, 1.0, 16.0))
    return {
        "x": x,
        "p": p,
        "ffn1_norm": gain((L, D_MODEL)),
        "ffn1_w_gate": nrm((L, D_MODEL, D_FF), D_MODEL ** -0.5),
        "ffn1_w_up": nrm((L, D_MODEL, D_FF), D_MODEL ** -0.5),
        "ffn1_w_down": nrm((L, D_FF, D_MODEL), D_FF ** -0.5),
        "mix_norm": gain((L, D_MODEL)),
        "w_in": nrm((L, D_MODEL, IN_PROJ_DIM), D_MODEL ** -0.5),
        "gm_ln_g": gain((L, GM_WIDTH)),
        "gm_ln_b": nrm((L, GM_WIDTH), 0.02),
        "gm_w_s": nrm((L, GM_HEADS, GM_CHUNK, GM_CHUNK), 0.5 * GM_CHUNK ** -0.5),
        "gm_b_s": gain((L, GM_HEADS, GM_CHUNK)),
        "gm_out_norm": gain((L, GM_WIDTH)),
        "conv_w": nrm((L, SSM_CONV, SSM_CONV_DIM), SSM_CONV ** -0.5),
        "conv_b": nrm((L, SSM_CONV_DIM), 0.02),
        "dt_bias": dt_bias,
        "a_log": a_log,
        "d_skip": gain((L, SSM_HEADS)),
        "ssm_norm": gain((L, SSM_WIDTH)),
        "w_out": nrm((L, D_MIX, D_MODEL), D_MIX ** -0.5),
        "ffn2_norm": gain((L, D_MODEL)),
        "ffn2_w_gate": nrm((L, D_MODEL, D_FF), D_MODEL ** -0.5),
        "ffn2_w_up": nrm((L, D_MODEL, D_FF), D_MODEL ** -0.5),
        "ffn2_w_down": nrm((L, D_FF, D_MODEL), D_FF ** -0.5),
        "ple_norm": gain((L, D_MODEL)),
        "ple_w_gate": nrm((L, D_MODEL, D_MODEL), D_MODEL ** -0.5),
        "ple_b_gate": nrm((L, D_MODEL), 0.02),
        "ple_w_proj": nrm((L, D_PLE, D_MODEL), D_PLE ** -0.5),
        "final_norm": gain((D_MODEL,)),
    }


def reference(x, p, ffn1_norm, ffn1_w_gate, ffn1_w_up, ffn1_w_down, mix_norm, w_in,
              gm_ln_g, gm_ln_b, gm_w_s, gm_b_s, gm_out_norm, conv_w, conv_b, dt_bias, a_log,
              d_skip, ssm_norm, w_out, ffn2_norm, ffn2_w_gate, ffn2_w_up, ffn2_w_down,
              ple_norm, ple_w_gate, ple_b_gate, ple_w_proj, final_norm):
    h = x
    for i in range(DEPTH):
        h = h + 0.5 * swiglu(rmsnorm(h, ffn1_norm[i]), ffn1_w_gate[i], ffn1_w_up[i], ffn1_w_down[i])
        n = rmsnorm(h, mix_norm[i])
        proj = n @ w_in[i]
        u, v, z, xbc, dt_raw = jnp.split(proj, IN_SPLITS, axis=-1)
        ya = chunked_spatial_gating(jax.nn.gelu(u, approximate=False), jax.nn.gelu(v, approximate=False),
                                    gm_ln_g[i], gm_ln_b[i], gm_w_s[i], gm_b_s[i])
        ya = rmsnorm(ya, gm_out_norm[i])
        yb = mamba2_mixer(z, xbc, dt_raw, conv_w[i], conv_b[i], dt_bias[i], a_log[i], d_skip[i], ssm_norm[i])
        h = h + jnp.concatenate([ya, yb], axis=-1) @ w_out[i]
        h = h + 0.5 * swiglu(rmsnorm(h, ffn2_norm[i]), ffn2_w_gate[i], ffn2_w_up[i], ffn2_w_down[i])
        gate = jax.nn.sigmoid(rmsnorm(h, ple_norm[i]) @ ple_w_gate[i] + ple_b_gate[i])
        h = h + gate * (p[i] @ ple_w_proj[i])
    return rmsnorm(h, final_norm)
```

```python
import functools
import math

import jax
import jax.numpy as jnp
import numpy as np
from jax import lax
from jax.experimental import pallas as pl
from jax.experimental.pallas import tpu as pltpu

F32 = jnp.float32
BF16 = jnp.bfloat16
EPS = 1e-6

LANES = 128
SUBLANES = 8
VMEM_LIMIT_BYTES = 56 * 1024 * 1024

GM_HEADS = 8
CHUNK = 128
SSM_HEAD_DIM = 64
SSM_GROUPS = 2
SSM_STATE = 128
SSM_CONV = 4

ROW_TILE = 512
FF_TILE = 256
SEQ_TILE = 256


def _dot(a, b):
    return jnp.dot(a, b, preferred_element_type=F32)


def _dot_exact(a, b):
    return jnp.dot(a, b, preferred_element_type=F32, precision=lax.Precision.HIGHEST)


def _rms(x, g):
    return x * lax.rsqrt(jnp.mean(x * x, axis=-1, keepdims=True) + EPS) * g


def _silu(x):
    return x * jax.nn.sigmoid(x)


def _gelu(x):
    return 0.5 * x * (1.0 + lax.erf(x * np.float32(math.sqrt(0.5))))


def _softplus(x):
    return jnp.maximum(x, 0.0) + jnp.log1p(jnp.exp(-jnp.abs(x)))


def _swiglu(xn, wg_ref, wu_ref, wd_ref, hid_ref):
    d_ff = hid_ref.shape[1]
    for j in range(d_ff // FF_TILE):
        sl = slice(j * FF_TILE, (j + 1) * FF_TILE)
        g = _dot(xn, wg_ref[:, sl])
        u = _dot(xn, wu_ref[:, sl])
        hid_ref[:, sl] = (_silu(g) * u).astype(BF16)
    return _dot(hid_ref[...], wd_ref[...])


def _ffn1_kernel(x_ref, g_ref, wg_ref, wu_ref, wd_ref, o_ref, hid_ref):
    x = x_ref[...]
    xn = _rms(x, g_ref[...]).astype(BF16)
    o_ref[...] = x + 0.5 * _swiglu(xn, wg_ref, wu_ref, wd_ref, hid_ref)


def _const_spec(shape):
    nd = len(shape)
    return pl.BlockSpec(shape, lambda *_: (0,) * nd, pipeline_mode=pl.Buffered(1))


def _row_spec(cols):
    return pl.BlockSpec((ROW_TILE, cols), lambda i: (i, 0))


def _ffn1(x, g, wg, wu, wd):
    t, d = x.shape
    d_ff = wg.shape[1]
    return pl.pallas_call(
        _ffn1_kernel,
        grid=(t // ROW_TILE,),
        in_specs=[_row_spec(d), _const_spec((1, d)), _const_spec((d, d_ff)), _const_spec((d, d_ff)),
                  _const_spec((d_ff, d))],
        out_specs=_row_spec(d),
        out_shape=jax.ShapeDtypeStruct((t, d), F32),
        scratch_shapes=[pltpu.VMEM((ROW_TILE, d_ff), BF16)],
        compiler_params=pltpu.CompilerParams(dimension_semantics=("parallel",),
                                             vmem_limit_bytes=VMEM_LIMIT_BYTES),
        name="ffn1",
    )(x, g, wg, wu, wd)


def _inproj_kernel(h_ref, g_ref, wu_ref, wv_ref, wz_ref, wx_ref, wdt_ref, lng_ref, lnb_ref, dtb_ref,
                   gu_ref, vn_ref, sz_ref, xbc_ref, dt_ref):
    n = _rms(h_ref[...], g_ref[...]).astype(BF16)
    gu_ref[...] = _gelu(_dot(n, wu_ref[...])).astype(BF16)
    v = _gelu(_dot(n, wv_ref[...]))
    vc = v - jnp.mean(v, axis=-1, keepdims=True)
    vn = vc * lax.rsqrt(jnp.mean(vc * vc, axis=-1, keepdims=True) + EPS)
    vn_ref[...] = (vn * lng_ref[...] + lnb_ref[...]).astype(BF16)
    sz_ref[...] = _silu(_dot(n, wz_ref[...])).astype(BF16)
    xbc_ref[...] = _dot(n, wx_ref[...]).astype(BF16)
    dt_ref[...] = _softplus(_dot(n, wdt_ref[...]) + dtb_ref[...])


def _inproj(h, g, wu, wv, wz, wx, wdt, lng, lnb, dtb):
    t, d = h.shape
    gm = wu.shape[1]
    sw = wz.shape[1]
    cd = wx.shape[1]
    outs = (jax.ShapeDtypeStruct((t, gm), BF16), jax.ShapeDtypeStruct((t, gm), BF16),
            jax.ShapeDtypeStruct((t, sw), BF16), jax.ShapeDtypeStruct((t, cd), BF16),
            jax.ShapeDtypeStruct((t, LANES), F32))
    return pl.pallas_call(
        _inproj_kernel,
        grid=(t // ROW_TILE,),
        in_specs=[_row_spec(d), _const_spec((1, d)), _const_spec((d, gm)), _const_spec((d, gm)),
                  _const_spec((d, sw)), _const_spec((d, cd)), _const_spec((d, LANES)),
                  _const_spec((1, gm)), _const_spec((1, gm)), _const_spec((1, LANES))],
        out_specs=(_row_spec(gm), _row_spec(gm), _row_spec(sw), _row_spec(cd), _row_spec(LANES)),
        out_shape=outs,
        compiler_params=pltpu.CompilerParams(dimension_semantics=("parallel",),
                                             vmem_limit_bytes=VMEM_LIMIT_BYTES),
        name="inproj",
    )(h, g, wu, wv, wz, wx, wdt, lng, lnb, dtb)


def _mixer_kernel(gu_ref, vn_ref, sz_ref, xbc_ref, dt_ref, ws_ref, bs_ref, gon_ref, cw_ref, cb_ref,
                  alog_ref, dsk_ref, sn_ref, y_ref, xe_ref, st_ref):
    gm = gu_ref.shape[1]
    sw = sz_ref.shape[1]
    grp = sw // SSM_GROUPS
    pairs_per_grp = grp // LANES

    @pl.when(pl.program_id(1) == 0)
    def _():
        xe_ref[0:SUBLANES, :] = jnp.zeros((SUBLANES, xe_ref.shape[1]), F32)
        st_ref[...] = jnp.zeros(st_ref.shape, F32)

    xe_ref[SUBLANES:SUBLANES + SEQ_TILE, :] = xbc_ref[...].astype(F32)
    conv = cb_ref[...] + cw_ref[SSM_CONV - 1:SSM_CONV, :] * xe_ref[SUBLANES:SUBLANES + SEQ_TILE, :]
    for s in range(1, SSM_CONV):
        conv = conv + cw_ref[SSM_CONV - 1 - s:SSM_CONV - s, :] * xe_ref[SUBLANES - s:SUBLANES - s + SEQ_TILE, :]
    xe_ref[0:SUBLANES, :] = xe_ref[SEQ_TILE:SEQ_TILE + SUBLANES, :]
    xc = _silu(conv)

    a_neg = -jnp.exp(alog_ref[...])
    row = lax.broadcasted_iota(jnp.int32, (CHUNK, CHUNK), 0)
    col = lax.broadcasted_iota(jnp.int32, (CHUNK, CHUNK), 1)
    causal = row >= col
    tri = causal.astype(F32)
    expand = (lax.broadcasted_iota(jnp.int32, (LANES, sw), 0)
              == lax.broadcasted_iota(jnp.int32, (LANES, sw), 1) // SSM_HEAD_DIM).astype(F32)
    low_half = col < SSM_HEAD_DIM

    for c in range(SEQ_TILE // CHUNK):
        r = slice(c * CHUNK, (c + 1) * CHUNK)

        ya = []
        for h in range(GM_HEADS):
            cs = slice(h * CHUNK, (h + 1) * CHUNK)
            mixed = _dot(ws_ref[h], vn_ref[r, cs]) + bs_ref[h]
            ya.append(gu_ref[r, cs].astype(F32) * mixed)
        ya = _rms(jnp.concatenate(ya, axis=1), gon_ref[...])
        y_ref[r, 0:gm] = ya.astype(BF16)

        xs = xc[r, 0:sw]
        dt = dt_ref[r, :]
        acs = _dot_exact(tri, dt * a_neg)
        acs_t = acs.T
        dt_x = _dot_exact(dt, expand)
        acs_x = _dot_exact(acs, expand)
        last_x = acs_x[CHUNK - 1:CHUNK, :]
        out_decay = jnp.exp(acs_x)
        state_decay = jnp.exp(last_x - acs_x)
        chunk_decay = jnp.exp(last_x)
        xdt = xs * dt_x
        xdt_b = xdt.astype(BF16)
        xdtd_b = (xdt * state_decay).astype(BF16)
        ys = []
        for g in range(SSM_GROUPS):
            b_g = xc[r, sw + g * SSM_STATE:sw + (g + 1) * SSM_STATE]
            c_g = xc[r, sw + (SSM_GROUPS + g) * SSM_STATE:sw + (SSM_GROUPS + g + 1) * SSM_STATE].astype(BF16)
            cb = lax.dot_general(c_g, b_g.astype(BF16), (((1,), (1,)), ((), ())), preferred_element_type=F32)
            gs = slice(g * grp, (g + 1) * grp)
            st = st_ref[:, gs]
            y_off = _dot(c_g, st.astype(BF16)) * out_decay[:, gs]
            st_ref[:, gs] = st * chunk_decay[:, gs] + _dot(b_g.T.astype(BF16), xdtd_b[:, gs])
            for j in range(pairs_per_grp):
                pair = g * pairs_per_grp + j
                ms = []
                for q in range(2):
                    k = 2 * pair + q
                    diff = acs[:, k:k + 1] - acs_t[k:k + 1, :]
                    ms.append((cb * jnp.exp(jnp.where(causal, diff, -jnp.inf))).astype(BF16))
                slab = xdt_b[:, pair * LANES:(pair + 1) * LANES]
                zero = jnp.zeros_like(slab)
                rhs = jnp.concatenate([jnp.where(low_half, slab, zero), jnp.where(low_half, zero, slab)], axis=0)
                y_diag = _dot(jnp.concatenate(ms, axis=1), rhs)
                ys.append(y_diag + y_off[:, j * LANES:(j + 1) * LANES])
        y = jnp.concatenate(ys, axis=1) + xs * dsk_ref[...]
        y = y * sz_ref[r, :].astype(F32)
        yn = []
        for g in range(SSM_GROUPS):
            yg = y[:, g * grp:(g + 1) * grp]
            yn.append(yg * lax.rsqrt(jnp.mean(yg * yg, axis=-1, keepdims=True) + EPS))
        y_ref[r, gm:gm + sw] = (jnp.concatenate(yn, axis=1) * sn_ref[...]).astype(BF16)


def _mixer(gu, vn, sz, xbc, dt, ws, bs, gon, cw, cb, alog, dsk, sn, batch, seq):
    t, gm = gu.shape
    sw = sz.shape[1]
    cd = xbc.shape[1]
    steps = seq // SEQ_TILE

    def seq_spec(cols):
        return pl.BlockSpec((SEQ_TILE, cols), lambda b, s: (b * steps + s, 0))

    return pl.pallas_call(
        _mixer_kernel,
        grid=(batch, steps),
        in_specs=[seq_spec(gm), seq_spec(gm), seq_spec(sw), seq_spec(cd), seq_spec(LANES),
                  _const_spec(ws.shape), _const_spec(bs.shape), _const_spec((1, gm)),
                  _const_spec(cw.shape), _const_spec((1, cd)), _const_spec((1, LANES)),
                  _const_spec((1, sw)), _const_spec((1, sw))],
        out_specs=seq_spec(gm + sw),
        out_shape=jax.ShapeDtypeStruct((t, gm + sw), BF16),
        scratch_shapes=[pltpu.VMEM((SEQ_TILE + SUBLANES, cd), F32), pltpu.VMEM((SSM_STATE, sw), F32)],
        compiler_params=pltpu.CompilerParams(dimension_semantics=("parallel", "arbitrary"),
                                             vmem_limit_bytes=VMEM_LIMIT_BYTES),
        name="mixer",
    )(gu, vn, sz, xbc, dt, ws, bs, gon, cw, cb, alog, dsk, sn)


def _tail_kernel(h_ref, y_ref, p_ref, wo_ref, g_ref, wg_ref, wu_ref, wd_ref, pn_ref, wpg_ref, bpg_ref, wpp_ref,
                 fn_ref, o_ref, hid_ref, *, apply_final_norm):
    h2 = h_ref[...] + _dot(y_ref[...], wo_ref[...])
    h3 = h2 + 0.5 * _swiglu(_rms(h2, g_ref[...]).astype(BF16), wg_ref, wu_ref, wd_ref, hid_ref)
    gate = jax.nn.sigmoid(_dot(_rms(h3, pn_ref[...]).astype(BF16), wpg_ref[...]) + bpg_ref[...])
    h4 = h3 + gate * _dot(p_ref[...].astype(BF16), wpp_ref[...])
    o_ref[...] = _rms(h4, fn_ref[...]) if apply_final_norm else h4


def _tail(h, y, p, wo, g, wg, wu, wd, pn, wpg, bpg, wpp, fn, apply_final_norm):
    t, d = h.shape
    d_ff = wg.shape[1]
    dm = y.shape[1]
    dp = p.shape[1]
    return pl.pallas_call(
        functools.partial(_tail_kernel, apply_final_norm=apply_final_norm),
        grid=(t // ROW_TILE,),
        in_specs=[_row_spec(d), _row_spec(dm), _row_spec(dp), _const_spec((dm, d)), _const_spec((1, d)),
                  _const_spec((d, d_ff)), _const_spec((d, d_ff)), _const_spec((d_ff, d)), _const_spec((1, d)),
                  _const_spec((d, d)), _const_spec((1, d)), _const_spec((dp, d)), _const_spec((1, d))],
        out_specs=_row_spec(d),
        out_shape=jax.ShapeDtypeStruct((t, d), F32),
        scratch_shapes=[pltpu.VMEM((ROW_TILE, d_ff), BF16)],
        compiler_params=pltpu.CompilerParams(dimension_semantics=("parallel",),
                                             vmem_limit_bytes=VMEM_LIMIT_BYTES),
        name="tail",
    )(h, y, p, wo, g, wg, wu, wd, pn, wpg, bpg, wpp, fn)


def _pad_lanes(v):
    return jnp.zeros((1, LANES), F32).at[0, :v.shape[0]].set(v.astype(F32))


def kernel(x, p, ffn1_norm, ffn1_w_gate, ffn1_w_up, ffn1_w_down, mix_norm, w_in, gm_ln_g, gm_ln_b, gm_w_s, gm_b_s,
           gm_out_norm, conv_w, conv_b, dt_bias, a_log, d_skip, ssm_norm, w_out, ffn2_norm, ffn2_w_gate, ffn2_w_up,
           ffn2_w_down, ple_norm, ple_w_gate, ple_b_gate, ple_w_proj, final_norm):
    batch, seq, d = x.shape
    depth = p.shape[0]
    gm = gm_ln_g.shape[1]
    sw = ssm_norm.shape[1]
    cd = conv_b.shape[1]
    heads = dt_bias.shape[1]
    assert seq % SEQ_TILE == 0 and (batch * seq) % ROW_TILE == 0 and heads <= LANES
    assert gm == GM_HEADS * CHUNK and sw == heads * SSM_HEAD_DIM and cd == sw + 2 * SSM_GROUPS * SSM_STATE

    row = lambda v: v.reshape(1, -1).astype(F32)
    causal = jnp.tril(jnp.ones((CHUNK, CHUNK), dtype=bool))
    h = x.reshape(batch * seq, d)
    for i in range(depth):
        h1 = _ffn1(h, row(ffn1_norm[i]), ffn1_w_gate[i].astype(BF16), ffn1_w_up[i].astype(BF16),
                   ffn1_w_down[i].astype(BF16))
        w = w_in[i].astype(BF16)
        w_dt = jnp.zeros((d, LANES), BF16).at[:, :heads].set(w[:, 2 * gm + sw + cd:])
        gu, vn, sz, xbc, dt = _inproj(
            h1, row(mix_norm[i]), w[:, :gm], w[:, gm:2 * gm], w[:, 2 * gm:2 * gm + sw],
            w[:, 2 * gm + sw:2 * gm + sw + cd], w_dt, row(gm_ln_g[i]), row(gm_ln_b[i]), _pad_lanes(dt_bias[i]))
        ws = jnp.where(causal, gm_w_s[i], 0.0).astype(BF16)
        bs = jnp.broadcast_to(gm_b_s[i].astype(F32)[:, :, None], (GM_HEADS, CHUNK, CHUNK))
        y = _mixer(gu, vn, sz, xbc, dt, ws, bs, row(gm_out_norm[i]), conv_w[i].astype(F32), row(conv_b[i]),
                   _pad_lanes(a_log[i]), row(jnp.repeat(d_skip[i], SSM_HEAD_DIM)), row(ssm_norm[i]), batch, seq)
        h = _tail(h1, y, p[i].reshape(batch * seq, -1), w_out[i].astype(BF16), row(ffn2_norm[i]),
                  ffn2_w_gate[i].astype(BF16), ffn2_w_up[i].astype(BF16), ffn2_w_down[i].astype(BF16),
                  row(ple_norm[i]), ple_w_gate[i].astype(BF16), row(ple_b_gate[i]), ple_w_proj[i].astype(BF16),
                  row(final_norm), apply_final_norm=(i == depth - 1))
    return h.reshape(batch, seq, d)
```

```python
import functools
import math

import jax
import jax.numpy as jnp
import numpy as np
from jax import lax
from jax.experimental import pallas as pl
from jax.experimental.pallas import tpu as pltpu

F32 = jnp.float32
BF16 = jnp.bfloat16
EPS = 1e-6

LANES = 128
SUBLANES = 8
VMEM_LIMIT_BYTES = 56 * 1024 * 1024

GM_HEADS = 8
CHUNK = 128
SSM_HEAD_DIM = 64
SSM_GROUPS = 2
SSM_STATE = 128
SSM_CONV = 4

ROW_TILE = 512
FF_TILE = 256
COL_TILE = 256


def _dot(a, b):
    return jnp.dot(a, b, preferred_element_type=F32)


def _unit_rms(x):
    return x * lax.rsqrt(jnp.mean(x * x, axis=-1, keepdims=True) + EPS)


def _silu(x):
    return x * jax.nn.sigmoid(x)


def _gelu(x):
    return 0.5 * x * (1.0 + lax.erf(x * np.float32(math.sqrt(0.5))))


def _softplus(x):
    return jnp.maximum(x, 0.0) + jnp.log1p(jnp.exp(-jnp.abs(x)))


def _split(x, parts):
    out = []
    for _ in range(parts - 1):
        piece = x.astype(BF16)
        out.append(piece)
        x = x - piece.astype(F32)
    out.append(x.astype(BF16))
    return jnp.concatenate(out, axis=1)


def _swiglu(xn, wg_ref, wu_ref, wd_ref, hid_ref):
    for j in range(hid_ref.shape[1] // FF_TILE):
        sl = slice(j * FF_TILE, (j + 1) * FF_TILE)
        g = _dot(xn, wg_ref[:, sl])
        u = _dot(xn, wu_ref[:, sl])
        hid_ref[:, sl] = (_silu(g) * u).astype(BF16)
    return _dot(hid_ref[...], wd_ref[...])


def _const_spec(shape):
    nd = len(shape)
    return pl.BlockSpec(shape, lambda *_: (0,) * nd, pipeline_mode=pl.Buffered(1))


def _row_spec(cols):
    return pl.BlockSpec((ROW_TILE, cols), lambda i: (i, 0))


def _ffn1_kernel(x_ref, wg_ref, wu_ref, wd_ref, o_ref, hid_ref):
    x = x_ref[...]
    o_ref[...] = x + 0.5 * _swiglu(_unit_rms(x).astype(BF16), wg_ref, wu_ref, wd_ref, hid_ref)


def _ffn1(x, wg, wu, wd):
    t, d = x.shape
    d_ff = wg.shape[1]
    return pl.pallas_call(
        _ffn1_kernel,
        grid=(t // ROW_TILE,),
        in_specs=[_row_spec(d), _const_spec((d, d_ff)), _const_spec((d, d_ff)), _const_spec((d_ff, d))],
        out_specs=_row_spec(d),
        out_shape=jax.ShapeDtypeStruct((t, d), F32),
        scratch_shapes=[pltpu.VMEM((ROW_TILE, d_ff), BF16)],
        compiler_params=pltpu.CompilerParams(dimension_semantics=("parallel",),
                                             vmem_limit_bytes=VMEM_LIMIT_BYTES),
        name="ffn1",
    )(x, wg, wu, wd)


def _inproj_kernel(h_ref, wu_ref, wv_ref, wz_ref, wx_ref, wdt_ref, lng_ref, lnb_ref, dtb_ref, cw_ref, cb_ref,
                   gu_ref, vn_ref, sz_ref, xc_ref, dt_ref, xe_ref, vb_ref, *, tiles_per_seq):
    @pl.when(pl.program_id(0) % tiles_per_seq == 0)
    def _():
        xe_ref[0:SUBLANES, :] = jnp.zeros((SUBLANES, xe_ref.shape[1]), F32)

    n = _unit_rms(h_ref[...]).astype(BF16)
    gm = gu_ref.shape[1]
    chunks = lambda width: [slice(j, j + COL_TILE) for j in range(0, width, COL_TILE)]
    conv_chunks = chunks(xc_ref.shape[1])

    for sl in conv_chunks:
        xe_ref[SUBLANES:SUBLANES + ROW_TILE, sl] = _dot(n, wx_ref[:, sl])

    def conv_steps():
        for sl in conv_chunks:
            xe = xe_ref[:, sl]
            p = pltpu.roll(xe, 1, axis=0)
            q = pltpu.roll(cw_ref[1:2, sl] * xe + cw_ref[0:1, sl] * p, 2, axis=0)
            conv = cb_ref[:, sl] + cw_ref[3:4, sl] * xe + cw_ref[2:3, sl] * p + q
            xc_ref[:, sl] = _silu(conv[SUBLANES:, :]).astype(BF16)
            xe_ref[0:SUBLANES, sl] = xe[ROW_TILE:, :]
            yield

    conv = conv_steps()

    row_sum = jnp.zeros((ROW_TILE, 1), F32)
    for sl in chunks(gm):
        v = _gelu(_dot(n, wv_ref[:, sl]))
        vb_ref[:, sl] = v
        row_sum = row_sum + jnp.sum(v, axis=-1, keepdims=True)
        next(conv, None)
    mean = row_sum * np.float32(1.0 / gm)
    sq_sum = jnp.zeros((ROW_TILE, 1), F32)
    for sl in chunks(gm):
        sz_ref[:, sl] = _silu(_dot(n, wz_ref[:, sl])).astype(BF16)
        vc = vb_ref[:, sl] - mean
        sq_sum = sq_sum + jnp.sum(vc * vc, axis=-1, keepdims=True)
        next(conv, None)
    for _ in conv:
        pass
    inv = lax.rsqrt(sq_sum * np.float32(1.0 / gm) + EPS)
    for sl in chunks(gm):
        gu_ref[:, sl] = _gelu(_dot(n, wu_ref[:, sl])).astype(BF16)
        vn_ref[:, sl] = ((vb_ref[:, sl] - mean) * inv * lng_ref[:, sl] + lnb_ref[:, sl]).astype(BF16)
    dt_ref[...] = _softplus(_dot(n, wdt_ref[...]) + dtb_ref[...])


def _inproj(h, wu, wv, wz, wx, wdt, lng, lnb, dtb, cw, cb, seq):
    t, d = h.shape
    gm = wu.shape[1]
    sw = wz.shape[1]
    cd = wx.shape[1]
    outs = (jax.ShapeDtypeStruct((t, gm), BF16), jax.ShapeDtypeStruct((t, gm), BF16),
            jax.ShapeDtypeStruct((t, sw), BF16), jax.ShapeDtypeStruct((t, cd), BF16),
            jax.ShapeDtypeStruct((t, LANES), F32))
    return pl.pallas_call(
        functools.partial(_inproj_kernel, tiles_per_seq=seq // ROW_TILE),
        grid=(t // ROW_TILE,),
        in_specs=[_row_spec(d), _const_spec((d, gm)), _const_spec((d, gm)), _const_spec((d, sw)),
                  _const_spec((d, cd)), _const_spec((d, LANES)), _const_spec((1, gm)), _const_spec((1, gm)),
                  _const_spec((1, LANES)), _const_spec(cw.shape), _const_spec((1, cd))],
        out_specs=(_row_spec(gm), _row_spec(gm), _row_spec(sw), _row_spec(cd), _row_spec(LANES)),
        out_shape=outs,
        scratch_shapes=[pltpu.VMEM((ROW_TILE + SUBLANES, cd), F32), pltpu.VMEM((ROW_TILE, gm), F32)],
        compiler_params=pltpu.CompilerParams(dimension_semantics=("arbitrary",),
                                             vmem_limit_bytes=VMEM_LIMIT_BYTES),
        name="inproj",
    )(h, wu, wv, wz, wx, wdt, lng, lnb, dtb, cw, cb)


def _mixer_kernel(gu_ref, vn_ref, sz_ref, xc_ref, dt_ref, ws_ref, bs_ref, alog_ref, dsk_ref, ex_ref, y_ref, st_ref):
    gm = gu_ref.shape[1]
    sw = sz_ref.shape[1]
    grp = sw // SSM_GROUPS
    pairs_per_grp = grp // LANES

    @pl.when(pl.program_id(1) == 0)
    def _():
        st_ref[...] = jnp.zeros(st_ref.shape, F32)

    a2 = -jnp.exp(alog_ref[...]) * np.float32(math.log2(math.e))
    row = lax.broadcasted_iota(jnp.int32, (CHUNK, CHUNK), 0)
    col = lax.broadcasted_iota(jnp.int32, (CHUNK, CHUNK), 1)
    causal = row >= col
    tri = causal.astype(BF16)
    low_half = col < SSM_HEAD_DIM

    for c in range(ROW_TILE // CHUNK):
        r = slice(c * CHUNK, (c + 1) * CHUNK)

        ya = []
        for h in range(GM_HEADS):
            cs = slice(h * CHUNK, (h + 1) * CHUNK)
            mixed = _dot(ws_ref[h], vn_ref[r, cs]) + bs_ref[h]
            ya.append(gu_ref[r, cs].astype(F32) * mixed)
        y_ref[r, 0:gm] = _unit_rms(jnp.concatenate(ya, axis=1)).astype(BF16)

        dt = dt_ref[r, :]
        cum = _dot(tri, _split(dt * a2, 3))
        acs = cum[:, 0:LANES] + cum[:, LANES:2 * LANES] + cum[:, 2 * LANES:3 * LANES]
        out_decay = jnp.exp2(acs)
        dt_state_decay = dt * jnp.exp2(acs[CHUNK - 1:CHUNK, :] - acs)
        src_t = (acs - jnp.log2(dt)).T
        spread = _dot(jnp.concatenate([_split(out_decay, 2), _split(dt_state_decay, 2)], axis=0), ex_ref[...])
        out_decay_x = spread[0:CHUNK, :]
        chunk_decay_x = out_decay_x[CHUNK - 1:CHUNK, :]
        xs_b = xc_ref[r, 0:sw]
        xs = xs_b.astype(F32)
        xw_b = (xs * spread[CHUNK:2 * CHUNK, :]).astype(BF16)
        ys = []
        for g in range(SSM_GROUPS):
            b_g = xc_ref[r, sw + g * SSM_STATE:sw + (g + 1) * SSM_STATE]
            c_g = xc_ref[r, sw + (SSM_GROUPS + g) * SSM_STATE:sw + (SSM_GROUPS + g + 1) * SSM_STATE]
            cb = lax.dot_general(c_g, b_g, (((1,), (1,)), ((), ())), preferred_element_type=F32)
            gs = slice(g * grp, (g + 1) * grp)
            st = st_ref[:, gs]
            y_off = _dot(c_g, st.astype(BF16)) * out_decay_x[:, gs]
            st_ref[:, gs] = st * chunk_decay_x[:, gs] + lax.dot_general(
                b_g, xw_b[:, gs], (((0,), (0,)), ((), ())), preferred_element_type=F32)
            for j in range(pairs_per_grp):
                pair = g * pairs_per_grp + j
                ms = []
                for q in range(2):
                    k = 2 * pair + q
                    diff = acs[:, k:k + 1] - src_t[k:k + 1, :]
                    ms.append((cb * jnp.exp2(jnp.where(causal, diff, -jnp.inf))).astype(BF16))
                slab = xs_b[:, pair * LANES:(pair + 1) * LANES]
                zero = jnp.zeros_like(slab)
                rhs = jnp.concatenate([jnp.where(low_half, slab, zero), jnp.where(low_half, zero, slab)], axis=0)
                y_diag = _dot(jnp.concatenate(ms, axis=1), rhs)
                ys.append(y_diag + y_off[:, j * LANES:(j + 1) * LANES])
        y = jnp.concatenate(ys, axis=1) + xs * dsk_ref[...]
        y = y * sz_ref[r, :].astype(F32)
        yn = [_unit_rms(y[:, g * grp:(g + 1) * grp]) for g in range(SSM_GROUPS)]
        y_ref[r, gm:gm + sw] = jnp.concatenate(yn, axis=1).astype(BF16)


def _mixer(gu, vn, sz, xc, dt, ws, bs, alog, dsk, ex, batch, seq):
    t, gm = gu.shape
    sw = sz.shape[1]
    cd = xc.shape[1]
    steps = seq // ROW_TILE

    def seq_spec(cols):
        return pl.BlockSpec((ROW_TILE, cols), lambda b, s: (b * steps + s, 0))

    return pl.pallas_call(
        _mixer_kernel,
        grid=(batch, steps),
        in_specs=[seq_spec(gm), seq_spec(gm), seq_spec(sw), seq_spec(cd), seq_spec(LANES),
                  _const_spec(ws.shape), _const_spec(bs.shape), _const_spec((1, LANES)), _const_spec((1, sw)),
                  _const_spec(ex.shape)],
        out_specs=seq_spec(gm + sw),
        out_shape=jax.ShapeDtypeStruct((t, gm + sw), BF16),
        scratch_shapes=[pltpu.VMEM((SSM_STATE, sw), F32)],
        compiler_params=pltpu.CompilerParams(dimension_semantics=("parallel", "arbitrary"),
                                             vmem_limit_bytes=VMEM_LIMIT_BYTES),
        name="mixer",
    )(gu, vn, sz, xc, dt, ws, bs, alog, dsk, ex)


def _tail_kernel(h_ref, y_ref, p_ref, wo_ref, wg_ref, wu_ref, wd_ref, wpg_ref, bpg_ref, wpp_ref, fn_ref, o_ref,
                 hid_ref, *, apply_final_norm):
    h2 = h_ref[...] + _dot(y_ref[...], wo_ref[...])
    h3 = h2 + 0.5 * _swiglu(_unit_rms(h2).astype(BF16), wg_ref, wu_ref, wd_ref, hid_ref)
    gate = jax.nn.sigmoid(_dot(_unit_rms(h3).astype(BF16), wpg_ref[...]) + bpg_ref[...])
    h4 = h3 + gate * _dot(p_ref[...].astype(BF16), wpp_ref[...])
    o_ref[...] = _unit_rms(h4) * fn_ref[...] if apply_final_norm else h4


def _tail(h, y, p, wo, wg, wu, wd, wpg, bpg, wpp, fn, apply_final_norm):
    t, d = h.shape
    d_ff = wg.shape[1]
    dm = y.shape[1]
    dp = p.shape[1]
    return pl.pallas_call(
        functools.partial(_tail_kernel, apply_final_norm=apply_final_norm),
        grid=(t // ROW_TILE,),
        in_specs=[_row_spec(d), _row_spec(dm), _row_spec(dp), _const_spec((dm, d)), _const_spec((d, d_ff)),
                  _const_spec((d, d_ff)), _const_spec((d_ff, d)), _const_spec((d, d)), _const_spec((1, d)),
                  _const_spec((dp, d)), _const_spec((1, d))],
        out_specs=_row_spec(d),
        out_shape=jax.ShapeDtypeStruct((t, d), F32),
        scratch_shapes=[pltpu.VMEM((ROW_TILE, d_ff), BF16)],
        compiler_params=pltpu.CompilerParams(dimension_semantics=("parallel",),
                                             vmem_limit_bytes=VMEM_LIMIT_BYTES),
        name="tail",
    )(h, y, p, wo, wg, wu, wd, wpg, bpg, wpp, fn)


def _pad_lanes(v):
    return jnp.zeros((1, LANES), F32).at[0, :v.shape[0]].set(v.astype(F32))


def _scaled(gain, w):
    return (gain.astype(F32)[:, None] * w.astype(F32)).astype(BF16)


def kernel(x, p, ffn1_norm, ffn1_w_gate, ffn1_w_up, ffn1_w_down, mix_norm, w_in, gm_ln_g, gm_ln_b, gm_w_s, gm_b_s,
           gm_out_norm, conv_w, conv_b, dt_bias, a_log, d_skip, ssm_norm, w_out, ffn2_norm, ffn2_w_gate, ffn2_w_up,
           ffn2_w_down, ple_norm, ple_w_gate, ple_b_gate, ple_w_proj, final_norm):
    batch, seq, d = x.shape
    depth = p.shape[0]
    gm = gm_ln_g.shape[1]
    sw = ssm_norm.shape[1]
    cd = conv_b.shape[1]
    heads = dt_bias.shape[1]
    assert seq % ROW_TILE == 0 and ROW_TILE % CHUNK == 0 and heads <= LANES and conv_w.shape[1] == SSM_CONV
    assert gm == sw and gm == GM_HEADS * CHUNK and sw == heads * SSM_HEAD_DIM
    assert cd == sw + 2 * SSM_GROUPS * SSM_STATE

    row = lambda v: v.reshape(1, -1).astype(F32)
    causal = jnp.tril(jnp.ones((CHUNK, CHUNK), dtype=bool))
    expand = (jnp.arange(LANES)[:, None] == jnp.arange(sw)[None, :] // SSM_HEAD_DIM).astype(BF16)
    expand2 = jnp.concatenate([expand, expand], axis=0)
    h = x.reshape(batch * seq, d)
    for i in range(depth):
        h1 = _ffn1(h, _scaled(ffn1_norm[i], ffn1_w_gate[i]), _scaled(ffn1_norm[i], ffn1_w_up[i]),
                   ffn1_w_down[i].astype(BF16))
        w = _scaled(mix_norm[i], w_in[i])
        w_dt = jnp.zeros((d, LANES), BF16).at[:, :heads].set(w[:, 2 * gm + sw + cd:])
        gu, vn, sz, xc, dt = _inproj(
            h1, w[:, :gm], w[:, gm:2 * gm], w[:, 2 * gm:2 * gm + sw], w[:, 2 * gm + sw:2 * gm + sw + cd], w_dt,
            row(gm_ln_g[i]), row(gm_ln_b[i]), _pad_lanes(dt_bias[i]), conv_w[i].astype(F32), row(conv_b[i]), seq)
        ws = jnp.where(causal, gm_w_s[i], 0.0).astype(BF16)
        bs = jnp.broadcast_to(gm_b_s[i].astype(F32)[:, :, None], (GM_HEADS, CHUNK, CHUNK))
        y = _mixer(gu, vn, sz, xc, dt, ws, bs, _pad_lanes(a_log[i]), row(jnp.repeat(d_skip[i], SSM_HEAD_DIM)),
                   expand2, batch, seq)
        h = _tail(h1, y, p[i].reshape(batch * seq, -1),
                  _scaled(jnp.concatenate([gm_out_norm[i], ssm_norm[i]]), w_out[i]),
                  _scaled(ffn2_norm[i], ffn2_w_gate[i]), _scaled(ffn2_norm[i], ffn2_w_up[i]),
                  ffn2_w_down[i].astype(BF16), _scaled(ple_norm[i], ple_w_gate[i]), row(ple_b_gate[i]),
                  ple_w_proj[i].astype(BF16), row(final_norm), apply_final_norm=(i == depth - 1))
    return h.reshape(batch, seq, d)
```

```python
import functools
import math

import jax
import jax.numpy as jnp
import numpy as np
from jax import lax
from jax.experimental import pallas as pl
from jax.experimental.pallas import tpu as pltpu

F32 = jnp.float32
BF16 = jnp.bfloat16
EPS = 1e-6

LANES = 128
SUBLANES = 8
VMEM_LIMIT_BYTES = 56 * 1024 * 1024

GM_HEADS = 8
CHUNK = 128
SSM_HEAD_DIM = 64
SSM_GROUPS = 2
SSM_STATE = 128
SSM_CONV = 4

ROW_TILE = 512
WIDE_TILE = 1024
FF_TILE = 256
COL_TILE = 256


def _dot(a, b):
    return jnp.dot(a, b, preferred_element_type=F32)


def _unit_rms(x):
    return x * lax.rsqrt(jnp.mean(x * x, axis=-1, keepdims=True) + EPS)


def _silu(x):
    return x * jax.nn.sigmoid(x)


def _gelu(x):
    return 0.5 * x * (1.0 + lax.erf(x * np.float32(math.sqrt(0.5))))


def _softplus(x):
    return jnp.maximum(x, 0.0) + jnp.log1p(jnp.exp(-jnp.abs(x)))


def _split(x, parts):
    out = []
    for _ in range(parts - 1):
        piece = x.astype(BF16)
        out.append(piece)
        x = x - piece.astype(F32)
    out.append(x.astype(BF16))
    return jnp.concatenate(out, axis=1)


def _anchor(ref, token, never):
    rows, cols = token.shape
    ref[0:rows, 0:cols] = jnp.where(never, token, ref[0:rows, 0:cols])


def _swiglu(xn_ref, wg_ref, wu_ref, wd_ref, hid_ref, side=None, never=None):
    for j in range(hid_ref.shape[1] // FF_TILE):
        sl = slice(j * FF_TILE, (j + 1) * FF_TILE)
        token = next(side, None) if side is not None and j else None
        if token is not None:
            _anchor(xn_ref, token, never)
        xn = xn_ref[...]
        g = _dot(xn, wg_ref[:, sl])
        u = _dot(xn, wu_ref[:, sl])
        hid_ref[:, sl] = (_silu(g) * u).astype(BF16)
    for _ in side or ():
        pass
    return _dot(hid_ref[...], wd_ref[...])


def _const_spec(shape):
    nd = len(shape)
    return pl.BlockSpec(shape, lambda *_: (0,) * nd, pipeline_mode=pl.Buffered(1))


def _row_spec(cols):
    return pl.BlockSpec((ROW_TILE, cols), lambda i: (i, 0))


def _ffn1_kernel(zero_ref, x_ref, wg_ref, wu_ref, wd_ref, wx_ref, cw_ref, cb_ref, h_ref, n_ref, xc_ref, hid_ref,
                 xn_ref, xe_ref, *, tiles_per_seq):
    i = pl.program_id(0)

    @pl.when(i == 0)
    def _():
        xe_ref[...] = jnp.zeros(xe_ref.shape, F32)

    @pl.when(jnp.logical_and(i > 0, (i - 1) % tiles_per_seq == 0))
    def _():
        xe_ref[0:SUBLANES, :] = jnp.zeros((SUBLANES, xe_ref.shape[1]), F32)

    conv_chunks = [slice(j, j + COL_TILE) for j in range(0, xc_ref.shape[1], COL_TILE)]
    def conv_steps():
        for sl in conv_chunks:
            xe = xe_ref[:, sl]
            p = pltpu.roll(xe, 1, axis=0)
            q = pltpu.roll(cw_ref[1:2, sl] * xe + cw_ref[0:1, sl] * p, 2, axis=0)
            conv = cb_ref[:, sl] + cw_ref[3:4, sl] * xe + cw_ref[2:3, sl] * p + q
            xc_ref[:, sl] = _silu(conv[SUBLANES:, :]).astype(BF16)
            xe_ref[0:SUBLANES, sl] = xe[ROW_TILE:, :]
            yield xc_ref[:, sl]

    x = x_ref[...]
    xn_ref[...] = _unit_rms(x).astype(BF16)
    h1 = x + 0.5 * _swiglu(xn_ref, wg_ref, wu_ref, wd_ref, hid_ref, side=conv_steps(), never=zero_ref[0] != 0)
    h_ref[...] = h1
    n = _unit_rms(h1).astype(BF16)
    n_ref[...] = n
    for sl in conv_chunks:
        xe_ref[SUBLANES:, sl] = _dot(n, wx_ref[:, sl])


def _ffn1(x, wg, wu, wd, wx, cw, cb, seq):
    t, d = x.shape
    d_ff = wg.shape[1]
    cd = wx.shape[1]
    tiles = t // ROW_TILE
    cur = lambda cols: pl.BlockSpec((ROW_TILE, cols), lambda i: (jnp.minimum(i, tiles - 1), 0))
    prev = lambda cols: pl.BlockSpec((ROW_TILE, cols), lambda i: (jnp.maximum(i - 1, 0), 0))
    return pl.pallas_call(
        functools.partial(_ffn1_kernel, tiles_per_seq=seq // ROW_TILE),
        grid=(tiles + 1,),
        in_specs=[pl.BlockSpec(memory_space=pltpu.SMEM), cur(d), _const_spec((d, d_ff)), _const_spec((d, d_ff)),
                  _const_spec((d_ff, d)), _const_spec((d, cd)), _const_spec(cw.shape), _const_spec((1, cd))],
        out_specs=(cur(d), cur(d), prev(cd)),
        out_shape=(jax.ShapeDtypeStruct((t, d), F32), jax.ShapeDtypeStruct((t, d), BF16),
                   jax.ShapeDtypeStruct((t, cd), BF16)),
        scratch_shapes=[pltpu.VMEM((ROW_TILE, d_ff), BF16), pltpu.VMEM((ROW_TILE, d), BF16),
                        pltpu.VMEM((SUBLANES + ROW_TILE, cd), F32)],
        compiler_params=pltpu.CompilerParams(dimension_semantics=("arbitrary",),
                                             vmem_limit_bytes=VMEM_LIMIT_BYTES),
        name="ffn1",
    )(jnp.zeros((1,), jnp.int32), x, wg, wu, wd, wx, cw, cb)


def _inproj_kernel(n_ref, wu_ref, wv_ref, wz_ref, wdt_ref, lng_ref, lnb_ref, dtb_ref,
                   gu_ref, vn_ref, sz_ref, dt_ref, vb_ref):
    n = n_ref[...]
    gm = gu_ref.shape[1]
    chunks = lambda width: [slice(j, j + COL_TILE) for j in range(0, width, COL_TILE)]

    row_sum = jnp.zeros((WIDE_TILE, 1), F32)
    for sl in chunks(gm):
        v = _gelu(_dot(n, wv_ref[:, sl]))
        vb_ref[:, sl] = v
        row_sum = row_sum + jnp.sum(v, axis=-1, keepdims=True)
    mean = row_sum * np.float32(1.0 / gm)
    sq_sum = jnp.zeros((WIDE_TILE, 1), F32)
    for sl in chunks(gm):
        sz_ref[:, sl] = _silu(_dot(n, wz_ref[:, sl])).astype(BF16)
        vc = vb_ref[:, sl] - mean
        sq_sum = sq_sum + jnp.sum(vc * vc, axis=-1, keepdims=True)
    inv = lax.rsqrt(sq_sum * np.float32(1.0 / gm) + EPS)
    for sl in chunks(gm):
        gu_ref[:, sl] = _gelu(_dot(n, wu_ref[:, sl])).astype(BF16)
        vn_ref[:, sl] = ((vb_ref[:, sl] - mean) * inv * lng_ref[:, sl] + lnb_ref[:, sl]).astype(BF16)
    dt_ref[...] = _softplus(_dot(n, wdt_ref[...]) + dtb_ref[...])


def _inproj(n, wu, wv, wz, wdt, lng, lnb, dtb):
    t, d = n.shape
    gm = wu.shape[1]
    sw = wz.shape[1]
    outs = (jax.ShapeDtypeStruct((t, gm), BF16), jax.ShapeDtypeStruct((t, gm), BF16),
            jax.ShapeDtypeStruct((t, sw), BF16), jax.ShapeDtypeStruct((t, LANES), F32))
    wide = lambda cols: pl.BlockSpec((WIDE_TILE, cols), lambda i: (i, 0))
    return pl.pallas_call(
        _inproj_kernel,
        grid=(t // WIDE_TILE,),
        in_specs=[wide(d), _const_spec((d, gm)), _const_spec((d, gm)), _const_spec((d, sw)),
                  _const_spec((d, LANES)), _const_spec((1, gm)), _const_spec((1, gm)), _const_spec((1, LANES))],
        out_specs=(wide(gm), wide(gm), wide(sw), wide(LANES)),
        out_shape=outs,
        scratch_shapes=[pltpu.VMEM((WIDE_TILE, gm), F32)],
        compiler_params=pltpu.CompilerParams(dimension_semantics=("parallel",),
                                             vmem_limit_bytes=VMEM_LIMIT_BYTES),
        name="inproj",
    )(n, wu, wv, wz, wdt, lng, lnb, dtb)


def _mixer_kernel(gu_ref, vn_ref, sz_ref, xc_ref, dt_ref, ws_ref, bs_ref, alog_ref, dsk_ref, ex_ref, y_ref, st_ref):
    gm = gu_ref.shape[1]
    sw = sz_ref.shape[1]
    grp = sw // SSM_GROUPS
    pairs_per_grp = grp // LANES

    @pl.when(pl.program_id(1) == 0)
    def _():
        st_ref[...] = jnp.zeros(st_ref.shape, F32)

    a2 = -jnp.exp(alog_ref[...]) * np.float32(math.log2(math.e))
    row = lax.broadcasted_iota(jnp.int32, (CHUNK, CHUNK), 0)
    col = lax.broadcasted_iota(jnp.int32, (CHUNK, CHUNK), 1)
    causal = row >= col
    tri = causal.astype(BF16)
    low_half = col < SSM_HEAD_DIM

    for c in range(WIDE_TILE // CHUNK):
        r = slice(c * CHUNK, (c + 1) * CHUNK)

        ya = []
        for h in range(GM_HEADS):
            cs = slice(h * CHUNK, (h + 1) * CHUNK)
            mixed = _dot(ws_ref[h], vn_ref[r, cs]) + bs_ref[h]
            ya.append(gu_ref[r, cs].astype(F32) * mixed)
        y_ref[r, 0:gm] = jnp.concatenate(ya, axis=1).astype(BF16)

        dt = dt_ref[r, :]
        cum = _dot(tri, _split(dt * a2, 3))
        acs = cum[:, 0:LANES] + cum[:, LANES:2 * LANES] + cum[:, 2 * LANES:3 * LANES]
        out_decay = jnp.exp2(acs)
        dt_state_decay = dt * jnp.exp2(acs[CHUNK - 1:CHUNK, :] - acs)
        src_t = (acs - jnp.log2(dt)).T
        spread = _dot(jnp.concatenate([_split(out_decay, 2), _split(dt_state_decay, 2)], axis=0), ex_ref[...])
        out_decay_x = spread[0:CHUNK, :]
        chunk_decay_x = out_decay_x[CHUNK - 1:CHUNK, :]
        xs_b = xc_ref[r, 0:sw]
        xs = xs_b.astype(F32)
        xw_b = (xs * spread[CHUNK:2 * CHUNK, :]).astype(BF16)
        ys = []
        for g in range(SSM_GROUPS):
            b_g = xc_ref[r, sw + g * SSM_STATE:sw + (g + 1) * SSM_STATE]
            c_g = xc_ref[r, sw + (SSM_GROUPS + g) * SSM_STATE:sw + (SSM_GROUPS + g + 1) * SSM_STATE]
            cb = lax.dot_general(c_g, b_g, (((1,), (1,)), ((), ())), preferred_element_type=F32)
            gs = slice(g * grp, (g + 1) * grp)
            st = st_ref[:, gs]
            y_off = _dot(c_g, st.astype(BF16)) * out_decay_x[:, gs]
            st_ref[:, gs] = st * chunk_decay_x[:, gs] + lax.dot_general(
                b_g, xw_b[:, gs], (((0,), (0,)), ((), ())), preferred_element_type=F32)
            for j in range(pairs_per_grp):
                pair = g * pairs_per_grp + j
                ms = []
                for q in range(2):
                    k = 2 * pair + q
                    diff = acs[:, k:k + 1] - src_t[k:k + 1, :]
                    ms.append((cb * jnp.exp2(jnp.where(causal, diff, -jnp.inf))).astype(BF16))
                slab = xs_b[:, pair * LANES:(pair + 1) * LANES]
                zero = jnp.zeros_like(slab)
                rhs = jnp.concatenate([jnp.where(low_half, slab, zero), jnp.where(low_half, zero, slab)], axis=0)
                y_diag = _dot(jnp.concatenate(ms, axis=1), rhs)
                ys.append(y_diag + y_off[:, j * LANES:(j + 1) * LANES])
        y = jnp.concatenate(ys, axis=1) + xs * dsk_ref[...]
        y_ref[r, gm:gm + sw] = (y * sz_ref[r, :].astype(F32)).astype(BF16)


def _mixer(gu, vn, sz, xc, dt, ws, bs, alog, dsk, ex, batch, seq):
    t, gm = gu.shape
    sw = sz.shape[1]
    cd = xc.shape[1]
    steps = seq // WIDE_TILE

    def seq_spec(cols):
        return pl.BlockSpec((WIDE_TILE, cols), lambda b, s: (b * steps + s, 0))

    return pl.pallas_call(
        _mixer_kernel,
        grid=(batch, steps),
        in_specs=[seq_spec(gm), seq_spec(gm), seq_spec(sw), seq_spec(cd), seq_spec(LANES),
                  _const_spec(ws.shape), _const_spec(bs.shape), _const_spec((1, LANES)), _const_spec((1, sw)),
                  _const_spec(ex.shape)],
        out_specs=seq_spec(gm + sw),
        out_shape=jax.ShapeDtypeStruct((t, gm + sw), BF16),
        scratch_shapes=[pltpu.VMEM((SSM_STATE, sw), F32)],
        compiler_params=pltpu.CompilerParams(dimension_semantics=("parallel", "arbitrary"),
                                             vmem_limit_bytes=VMEM_LIMIT_BYTES),
        name="mixer",
    )(gu, vn, sz, xc, dt, ws, bs, alog, dsk, ex)


def _tail_kernel(h_ref, y_ref, p_ref, wo_ref, wg_ref, wu_ref, wd_ref, wpg_ref, bpg_ref, wpp_ref, fn_ref, o_ref,
                 hid_ref, xn_ref, *, norm_groups, apply_final_norm):
    h2 = h_ref[...]
    for lo, hi in norm_groups:
        yg = y_ref[:, lo:hi]
        yf = yg.astype(F32)
        scale = lax.rsqrt(jnp.mean(yf * yf, axis=-1, keepdims=True) + EPS)
        h2 = h2 + scale * _dot(yg, wo_ref[lo:hi, :])
    xn_ref[...] = _unit_rms(h2).astype(BF16)
    h3 = h2 + 0.5 * _swiglu(xn_ref, wg_ref, wu_ref, wd_ref, hid_ref)
    gate = jax.nn.sigmoid(_dot(_unit_rms(h3).astype(BF16), wpg_ref[...]) + bpg_ref[...])
    h4 = h3 + gate * _dot(p_ref[...].astype(BF16), wpp_ref[...])
    o_ref[...] = _unit_rms(h4) * fn_ref[...] if apply_final_norm else h4


def _tail(h, y, p, wo, wg, wu, wd, wpg, bpg, wpp, fn, norm_groups, apply_final_norm):
    t, d = h.shape
    d_ff = wg.shape[1]
    dm = y.shape[1]
    dp = p.shape[1]
    return pl.pallas_call(
        functools.partial(_tail_kernel, norm_groups=norm_groups, apply_final_norm=apply_final_norm),
        grid=(t // ROW_TILE,),
        in_specs=[_row_spec(d), _row_spec(dm), _row_spec(dp), _const_spec((dm, d)), _const_spec((d, d_ff)),
                  _const_spec((d, d_ff)), _const_spec((d_ff, d)), _const_spec((d, d)), _const_spec((1, d)),
                  _const_spec((dp, d)), _const_spec((1, d))],
        out_specs=_row_spec(d),
        out_shape=jax.ShapeDtypeStruct((t, d), F32),
        scratch_shapes=[pltpu.VMEM((ROW_TILE, d_ff), BF16), pltpu.VMEM((ROW_TILE, d), BF16)],
        compiler_params=pltpu.CompilerParams(dimension_semantics=("parallel",),
                                             vmem_limit_bytes=VMEM_LIMIT_BYTES),
        name="tail",
    )(h, y, p, wo, wg, wu, wd, wpg, bpg, wpp, fn)


def _pad_lanes(v):
    return jnp.zeros((1, LANES), F32).at[0, :v.shape[0]].set(v.astype(F32))


def _scaled(gain, w):
    return (gain.astype(F32)[:, None] * w.astype(F32)).astype(BF16)


def kernel(x, p, ffn1_norm, ffn1_w_gate, ffn1_w_up, ffn1_w_down, mix_norm, w_in, gm_ln_g, gm_ln_b, gm_w_s, gm_b_s,
           gm_out_norm, conv_w, conv_b, dt_bias, a_log, d_skip, ssm_norm, w_out, ffn2_norm, ffn2_w_gate, ffn2_w_up,
           ffn2_w_down, ple_norm, ple_w_gate, ple_b_gate, ple_w_proj, final_norm):
    batch, seq, d = x.shape
    depth = p.shape[0]
    gm = gm_ln_g.shape[1]
    sw = ssm_norm.shape[1]
    cd = conv_b.shape[1]
    heads = dt_bias.shape[1]
    assert seq % ROW_TILE == 0 and seq % WIDE_TILE == 0 and WIDE_TILE % CHUNK == 0 and heads <= LANES and conv_w.shape[1] == SSM_CONV
    assert gm == sw and gm == GM_HEADS * CHUNK and sw == heads * SSM_HEAD_DIM
    assert cd == sw + 2 * SSM_GROUPS * SSM_STATE

    row = lambda v: v.reshape(1, -1).astype(F32)
    causal = jnp.tril(jnp.ones((CHUNK, CHUNK), dtype=bool))
    expand = (jnp.arange(LANES)[:, None] == jnp.arange(sw)[None, :] // SSM_HEAD_DIM).astype(BF16)
    expand2 = jnp.concatenate([expand, expand], axis=0)
    grp = sw // SSM_GROUPS
    norm_groups = ((0, gm),) + tuple((gm + g * grp, gm + (g + 1) * grp) for g in range(SSM_GROUPS))
    h = x.reshape(batch * seq, d)
    for i in range(depth):
        w = _scaled(mix_norm[i], w_in[i])
        h1, n, xc = _ffn1(h, _scaled(ffn1_norm[i], ffn1_w_gate[i]), _scaled(ffn1_norm[i], ffn1_w_up[i]),
                          ffn1_w_down[i].astype(BF16), w[:, 2 * gm + sw:2 * gm + sw + cd], conv_w[i].astype(F32),
                          row(conv_b[i]), seq)
        w_dt = jnp.zeros((d, LANES), BF16).at[:, :heads].set(w[:, 2 * gm + sw + cd:])
        gu, vn, sz, dt = _inproj(n, w[:, :gm], w[:, gm:2 * gm], w[:, 2 * gm:2 * gm + sw], w_dt,
                                 row(gm_ln_g[i]), row(gm_ln_b[i]), _pad_lanes(dt_bias[i]))
        ws = jnp.where(causal, gm_w_s[i], 0.0).astype(BF16)
        bs = jnp.broadcast_to(gm_b_s[i].astype(F32)[:, :, None], (GM_HEADS, CHUNK, CHUNK))
        y = _mixer(gu, vn, sz, xc, dt, ws, bs, _pad_lanes(a_log[i]), row(jnp.repeat(d_skip[i], SSM_HEAD_DIM)),
                   expand2, batch, seq)
        h = _tail(h1, y, p[i].reshape(batch * seq, -1),
                  _scaled(jnp.concatenate([gm_out_norm[i], ssm_norm[i]]), w_out[i]),
                  _scaled(ffn2_norm[i], ffn2_w_gate[i]), _scaled(ffn2_norm[i], ffn2_w_up[i]),
                  ffn2_w_down[i].astype(BF16), _scaled(ple_norm[i], ple_w_gate[i]), row(ple_b_gate[i]),
                  ple_w_proj[i].astype(BF16), row(final_norm), norm_groups=norm_groups,
                  apply_final_norm=(i == depth - 1))
    return h.reshape(batch, seq, d)
```

```python
import functools
import math

import jax
import jax.numpy as jnp
import numpy as np
from jax import lax
from jax.experimental import pallas as pl
from jax.experimental.pallas import tpu as pltpu

F32 = jnp.float32
BF16 = jnp.bfloat16
EPS = 1e-6

LANES = 128
SUBLANES = 8
VMEM_LIMIT_BYTES = 56 * 1024 * 1024

GM_HEADS = 8
CHUNK = 128
SSM_HEAD_DIM = 64
SSM_GROUPS = 2
SSM_STATE = 128
SSM_CONV = 4

ROW_TILE = 512
WIDE_TILE = 1024
FF_TILE = 256
COL_TILE = 256


def _dot(a, b):
    return jnp.dot(a, b, preferred_element_type=F32)


def _unit_rms(x):
    return x * lax.rsqrt(jnp.mean(x * x, axis=-1, keepdims=True) + EPS)


def _silu(x):
    return x * jax.nn.sigmoid(x)


def _gelu(x):
    return 0.5 * x * (1.0 + lax.erf(x * np.float32(math.sqrt(0.5))))


def _softplus(x):
    return jnp.maximum(x, 0.0) + jnp.log1p(jnp.exp(-jnp.abs(x)))


def _split(x, parts):
    out = []
    for _ in range(parts - 1):
        piece = x.astype(BF16)
        out.append(piece)
        x = x - piece.astype(F32)
    out.append(x.astype(BF16))
    return jnp.concatenate(out, axis=1)


def _anchor(ref, token, never):
    rows, cols = token.shape
    ref[0:rows, 0:cols] = jnp.where(never, token, ref[0:rows, 0:cols])


def _inv_rms(x):
    return lax.rsqrt(jnp.mean(x * x, axis=-1, keepdims=True) + EPS)


def _swiglu(xb_ref, scale, wg_ref, wu_ref, wd_ref, hid_ref, side=None, never=None):
    for j in range(hid_ref.shape[1] // FF_TILE):
        sl = slice(j * FF_TILE, (j + 1) * FF_TILE)
        token = next(side, None) if side is not None and j else None
        if token is not None:
            _anchor(xb_ref, token, never)
        xb = xb_ref[...]
        g = _dot(xb, wg_ref[:, sl])
        u = _dot(xb, wu_ref[:, sl])
        if scale is not None:
            g, u = g * scale, u * scale
        hid_ref[:, sl] = (_silu(g) * u).astype(BF16)
    for _ in side or ():
        pass
    return _dot(hid_ref[...], wd_ref[...])


def _const_spec(shape):
    nd = len(shape)
    return pl.BlockSpec(shape, lambda *_: (0,) * nd, pipeline_mode=pl.Buffered(1))


def _row_spec(cols):
    return pl.BlockSpec((ROW_TILE, cols), lambda i: (i, 0))


def _ffn1_kernel(zero_ref, x_ref, wg_ref, wu_ref, wd_ref, wx_ref, cw_ref, cb_ref, h_ref, n_ref, xc_ref, hid_ref,
                 xn_ref, xe_ref, *, tiles_per_seq):
    i = pl.program_id(0)

    @pl.when(i == 0)
    def _():
        xe_ref[...] = jnp.zeros(xe_ref.shape, F32)

    @pl.when(jnp.logical_and(i > 0, (i - 1) % tiles_per_seq == 0))
    def _():
        xe_ref[0:SUBLANES, :] = jnp.zeros((SUBLANES, xe_ref.shape[1]), F32)

    conv_chunks = [slice(j, j + COL_TILE) for j in range(0, xc_ref.shape[1], COL_TILE)]
    def conv_steps():
        for sl in conv_chunks:
            xe = xe_ref[:, sl]
            p = pltpu.roll(xe, 1, axis=0)
            q = pltpu.roll(cw_ref[1:2, sl] * xe + cw_ref[0:1, sl] * p, 2, axis=0)
            conv = cb_ref[:, sl] + cw_ref[3:4, sl] * xe + cw_ref[2:3, sl] * p + q
            xc_ref[:, sl] = _silu(conv[SUBLANES:, :]).astype(BF16)
            xe_ref[0:SUBLANES, sl] = xe[ROW_TILE:, :]
            yield xc_ref[:, sl]

    x = x_ref[...]
    xn_ref[...] = _unit_rms(x).astype(BF16)
    h1 = x + 0.5 * _swiglu(xn_ref, None, wg_ref, wu_ref, wd_ref, hid_ref, side=conv_steps(),
                           never=zero_ref[0] != 0)
    h_ref[...] = h1
    n = _unit_rms(h1).astype(BF16)
    n_ref[...] = n
    for sl in conv_chunks:
        xe_ref[SUBLANES:, sl] = _dot(n, wx_ref[:, sl])


def _ffn1(x, wg, wu, wd, wx, cw, cb, seq):
    t, d = x.shape
    d_ff = wg.shape[1]
    cd = wx.shape[1]
    tiles = t // ROW_TILE
    cur = lambda cols: pl.BlockSpec((ROW_TILE, cols), lambda i: (jnp.minimum(i, tiles - 1), 0))
    prev = lambda cols: pl.BlockSpec((ROW_TILE, cols), lambda i: (jnp.maximum(i - 1, 0), 0))
    return pl.pallas_call(
        functools.partial(_ffn1_kernel, tiles_per_seq=seq // ROW_TILE),
        grid=(tiles + 1,),
        in_specs=[pl.BlockSpec(memory_space=pltpu.SMEM), cur(d), _const_spec((d, d_ff)), _const_spec((d, d_ff)),
                  _const_spec((d_ff, d)), _const_spec((d, cd)), _const_spec(cw.shape), _const_spec((1, cd))],
        out_specs=(cur(d), cur(d), prev(cd)),
        out_shape=(jax.ShapeDtypeStruct((t, d), F32), jax.ShapeDtypeStruct((t, d), BF16),
                   jax.ShapeDtypeStruct((t, cd), BF16)),
        scratch_shapes=[pltpu.VMEM((ROW_TILE, d_ff), BF16), pltpu.VMEM((ROW_TILE, d), BF16),
                        pltpu.VMEM((SUBLANES + ROW_TILE, cd), F32)],
        compiler_params=pltpu.CompilerParams(dimension_semantics=("arbitrary",),
                                             vmem_limit_bytes=VMEM_LIMIT_BYTES),
        name="ffn1",
    )(jnp.zeros((1,), jnp.int32), x, wg, wu, wd, wx, cw, cb)


def _inproj_kernel(n_ref, wu_ref, wv_ref, wz_ref, wdt_ref, lng_ref, lnb_ref, dtb_ref,
                   gu_ref, vn_ref, sz_ref, dt_ref, vb_ref):
    n = n_ref[...]
    gm = gu_ref.shape[1]
    chunks = lambda width: [slice(j, j + COL_TILE) for j in range(0, width, COL_TILE)]

    row_sum = jnp.zeros((WIDE_TILE, 1), F32)
    for sl in chunks(gm):
        v = _gelu(_dot(n, wv_ref[:, sl]))
        vb_ref[:, sl] = v
        row_sum = row_sum + jnp.sum(v, axis=-1, keepdims=True)
    mean = row_sum * np.float32(1.0 / gm)
    sq_sum = jnp.zeros((WIDE_TILE, 1), F32)
    for sl in chunks(gm):
        sz_ref[:, sl] = _silu(_dot(n, wz_ref[:, sl])).astype(BF16)
        vc = vb_ref[:, sl] - mean
        sq_sum = sq_sum + jnp.sum(vc * vc, axis=-1, keepdims=True)
    inv = lax.rsqrt(sq_sum * np.float32(1.0 / gm) + EPS)
    for sl in chunks(gm):
        gu_ref[:, sl] = _gelu(_dot(n, wu_ref[:, sl])).astype(BF16)
        vn_ref[:, sl] = ((vb_ref[:, sl] - mean) * inv * lng_ref[:, sl] + lnb_ref[:, sl]).astype(BF16)
    dt_ref[...] = _softplus(_dot(n, wdt_ref[...]) + dtb_ref[...])


def _inproj(n, wu, wv, wz, wdt, lng, lnb, dtb):
    t, d = n.shape
    gm = wu.shape[1]
    sw = wz.shape[1]
    outs = (jax.ShapeDtypeStruct((t, gm), BF16), jax.ShapeDtypeStruct((t, gm), BF16),
            jax.ShapeDtypeStruct((t, sw), BF16), jax.ShapeDtypeStruct((t, LANES), F32))
    wide = lambda cols: pl.BlockSpec((WIDE_TILE, cols), lambda i: (i, 0))
    return pl.pallas_call(
        _inproj_kernel,
        grid=(t // WIDE_TILE,),
        in_specs=[wide(d), _const_spec((d, gm)), _const_spec((d, gm)), _const_spec((d, sw)),
                  _const_spec((d, LANES)), _const_spec((1, gm)), _const_spec((1, gm)), _const_spec((1, LANES))],
        out_specs=(wide(gm), wide(gm), wide(sw), wide(LANES)),
        out_shape=outs,
        scratch_shapes=[pltpu.VMEM((WIDE_TILE, gm), F32)],
        compiler_params=pltpu.CompilerParams(dimension_semantics=("parallel",),
                                             vmem_limit_bytes=VMEM_LIMIT_BYTES),
        name="inproj",
    )(n, wu, wv, wz, wdt, lng, lnb, dtb)


def _mixer_kernel(gu_ref, vn_ref, sz_ref, xc_ref, dt_ref, ws_ref, bs_ref, alog_ref, dsk_ref, ex_ref, y_ref, st_ref):
    gm = gu_ref.shape[1]
    sw = sz_ref.shape[1]
    grp = sw // SSM_GROUPS
    pairs_per_grp = grp // LANES

    @pl.when(pl.program_id(1) == 0)
    def _():
        st_ref[...] = jnp.zeros(st_ref.shape, F32)

    a2 = -jnp.exp(alog_ref[...]) * np.float32(math.log2(math.e))
    row = lax.broadcasted_iota(jnp.int32, (CHUNK, CHUNK), 0)
    col = lax.broadcasted_iota(jnp.int32, (CHUNK, CHUNK), 1)
    causal = row >= col
    tri = causal.astype(BF16)
    low_half = col < SSM_HEAD_DIM

    for c in range(WIDE_TILE // CHUNK):
        r = slice(c * CHUNK, (c + 1) * CHUNK)

        ya = []
        for h in range(GM_HEADS):
            cs = slice(h * CHUNK, (h + 1) * CHUNK)
            mixed = _dot(ws_ref[h], vn_ref[r, cs]) + bs_ref[h]
            ya.append(gu_ref[r, cs].astype(F32) * mixed)
        y_ref[r, 0:gm] = jnp.concatenate(ya, axis=1).astype(BF16)

        dt = dt_ref[r, :]
        cum = _dot(tri, _split(dt * a2, 3))
        acs = cum[:, 0:LANES] + cum[:, LANES:2 * LANES] + cum[:, 2 * LANES:3 * LANES]
        out_decay = jnp.exp2(acs)
        dt_state_decay = dt * jnp.exp2(acs[CHUNK - 1:CHUNK, :] - acs)
        src_t = (acs - jnp.log2(dt)).T
        spread = _dot(jnp.concatenate([_split(out_decay, 2), _split(dt_state_decay, 2)], axis=0), ex_ref[...])
        out_decay_x = spread[0:CHUNK, :]
        chunk_decay_x = out_decay_x[CHUNK - 1:CHUNK, :]
        xs_b = xc_ref[r, 0:sw]
        xs = xs_b.astype(F32)
        xw_b = (xs * spread[CHUNK:2 * CHUNK, :]).astype(BF16)
        ys = []
        for g in range(SSM_GROUPS):
            b_g = xc_ref[r, sw + g * SSM_STATE:sw + (g + 1) * SSM_STATE]
            c_g = xc_ref[r, sw + (SSM_GROUPS + g) * SSM_STATE:sw + (SSM_GROUPS + g + 1) * SSM_STATE]
            cb = lax.dot_general(c_g, b_g, (((1,), (1,)), ((), ())), preferred_element_type=F32)
            gs = slice(g * grp, (g + 1) * grp)
            st = st_ref[:, gs]
            y_off = _dot(c_g, st.astype(BF16)) * out_decay_x[:, gs]
            st_ref[:, gs] = st * chunk_decay_x[:, gs] + lax.dot_general(
                b_g, xw_b[:, gs], (((0,), (0,)), ((), ())), preferred_element_type=F32)
            for j in range(pairs_per_grp):
                pair = g * pairs_per_grp + j
                ms = []
                for q in range(2):
                    k = 2 * pair + q
                    diff = acs[:, k:k + 1] - src_t[k:k + 1, :]
                    ms.append((cb * jnp.exp2(jnp.where(causal, diff, -jnp.inf))).astype(BF16))
                slab = xs_b[:, pair * LANES:(pair + 1) * LANES]
                zero = jnp.zeros_like(slab)
                rhs = jnp.concatenate([jnp.where(low_half, slab, zero), jnp.where(low_half, zero, slab)], axis=0)
                y_diag = _dot(jnp.concatenate(ms, axis=1), rhs)
                ys.append(y_diag + y_off[:, j * LANES:(j + 1) * LANES])
        y = jnp.concatenate(ys, axis=1) + xs * dsk_ref[...]
        y_ref[r, gm:gm + sw] = (y * sz_ref[r, :].astype(F32)).astype(BF16)


def _mixer(gu, vn, sz, xc, dt, ws, bs, alog, dsk, ex, batch, seq):
    t, gm = gu.shape
    sw = sz.shape[1]
    cd = xc.shape[1]
    steps = seq // WIDE_TILE

    def seq_spec(cols):
        return pl.BlockSpec((WIDE_TILE, cols), lambda b, s: (b * steps + s, 0))

    return pl.pallas_call(
        _mixer_kernel,
        grid=(batch, steps),
        in_specs=[seq_spec(gm), seq_spec(gm), seq_spec(sw), seq_spec(cd), seq_spec(LANES),
                  _const_spec(ws.shape), _const_spec(bs.shape), _const_spec((1, LANES)), _const_spec((1, sw)),
                  _const_spec(ex.shape)],
        out_specs=seq_spec(gm + sw),
        out_shape=jax.ShapeDtypeStruct((t, gm + sw), BF16),
        scratch_shapes=[pltpu.VMEM((SSM_STATE, sw), F32)],
        compiler_params=pltpu.CompilerParams(dimension_semantics=("parallel", "arbitrary"),
                                             vmem_limit_bytes=VMEM_LIMIT_BYTES),
        name="mixer",
    )(gu, vn, sz, xc, dt, ws, bs, alog, dsk, ex)


def _tail_kernel(h_ref, y_ref, p_ref, wo_ref, wg_ref, wu_ref, wd_ref, wpg_ref, bpg_ref, wpp_ref, fn_ref, o_ref,
                 hid_ref, xn_ref, *, norm_groups, apply_final_norm):
    h2 = h_ref[...]
    for lo, hi in norm_groups:
        yg = y_ref[:, lo:hi]
        h2 = h2 + _inv_rms(yg.astype(F32)) * _dot(yg, wo_ref[lo:hi, :])
    xn_ref[...] = h2.astype(BF16)
    h3 = h2 + 0.5 * _swiglu(xn_ref, _inv_rms(h2), wg_ref, wu_ref, wd_ref, hid_ref)
    gate = jax.nn.sigmoid(_dot(h3.astype(BF16), wpg_ref[...]) * _inv_rms(h3) + bpg_ref[...])
    h4 = h3 + gate * _dot(p_ref[...].astype(BF16), wpp_ref[...])
    o_ref[...] = _unit_rms(h4) * fn_ref[...] if apply_final_norm else h4


def _tail(h, y, p, wo, wg, wu, wd, wpg, bpg, wpp, fn, norm_groups, apply_final_norm):
    t, d = h.shape
    d_ff = wg.shape[1]
    dm = y.shape[1]
    dp = p.shape[1]
    return pl.pallas_call(
        functools.partial(_tail_kernel, norm_groups=norm_groups, apply_final_norm=apply_final_norm),
        grid=(t // ROW_TILE,),
        in_specs=[_row_spec(d), _row_spec(dm), _row_spec(dp), _const_spec((dm, d)), _const_spec((d, d_ff)),
                  _const_spec((d, d_ff)), _const_spec((d_ff, d)), _const_spec((d, d)), _const_spec((1, d)),
                  _const_spec((dp, d)), _const_spec((1, d))],
        out_specs=_row_spec(d),
        out_shape=jax.ShapeDtypeStruct((t, d), F32),
        scratch_shapes=[pltpu.VMEM((ROW_TILE, d_ff), BF16), pltpu.VMEM((ROW_TILE, d), BF16)],
        compiler_params=pltpu.CompilerParams(dimension_semantics=("parallel",),
                                             vmem_limit_bytes=VMEM_LIMIT_BYTES),
        name="tail",
    )(h, y, p, wo, wg, wu, wd, wpg, bpg, wpp, fn)


def _pad_lanes(v):
    return jnp.zeros((1, LANES), F32).at[0, :v.shape[0]].set(v.astype(F32))


def _scaled(gain, w):
    return (gain.astype(F32)[:, None] * w.astype(F32)).astype(BF16)


def kernel(x, p, ffn1_norm, ffn1_w_gate, ffn1_w_up, ffn1_w_down, mix_norm, w_in, gm_ln_g, gm_ln_b, gm_w_s, gm_b_s,
           gm_out_norm, conv_w, conv_b, dt_bias, a_log, d_skip, ssm_norm, w_out, ffn2_norm, ffn2_w_gate, ffn2_w_up,
           ffn2_w_down, ple_norm, ple_w_gate, ple_b_gate, ple_w_proj, final_norm):
    batch, seq, d = x.shape
    depth = p.shape[0]
    gm = gm_ln_g.shape[1]
    sw = ssm_norm.shape[1]
    cd = conv_b.shape[1]
    heads = dt_bias.shape[1]
    assert seq % ROW_TILE == 0 and seq % WIDE_TILE == 0 and WIDE_TILE % CHUNK == 0 and heads <= LANES and conv_w.shape[1] == SSM_CONV
    assert gm == sw and gm == GM_HEADS * CHUNK and sw == heads * SSM_HEAD_DIM
    assert cd == sw + 2 * SSM_GROUPS * SSM_STATE

    row = lambda v: v.reshape(1, -1).astype(F32)
    causal = jnp.tril(jnp.ones((CHUNK, CHUNK), dtype=bool))
    expand = (jnp.arange(LANES)[:, None] == jnp.arange(sw)[None, :] // SSM_HEAD_DIM).astype(BF16)
    expand2 = jnp.concatenate([expand, expand], axis=0)
    grp = sw // SSM_GROUPS
    norm_groups = ((0, gm),) + tuple((gm + g * grp, gm + (g + 1) * grp) for g in range(SSM_GROUPS))
    h = x.reshape(batch * seq, d)
    for i in range(depth):
        w = _scaled(mix_norm[i], w_in[i])
        h1, n, xc = _ffn1(h, _scaled(ffn1_norm[i], ffn1_w_gate[i]), _scaled(ffn1_norm[i], ffn1_w_up[i]),
                          ffn1_w_down[i].astype(BF16), w[:, 2 * gm + sw:2 * gm + sw + cd], conv_w[i].astype(F32),
                          row(conv_b[i]), seq)
        w_dt = jnp.zeros((d, LANES), BF16).at[:, :heads].set(w[:, 2 * gm + sw + cd:])
        gu, vn, sz, dt = _inproj(n, w[:, :gm], w[:, gm:2 * gm], w[:, 2 * gm:2 * gm + sw], w_dt,
                                 row(gm_ln_g[i]), row(gm_ln_b[i]), _pad_lanes(dt_bias[i]))
        ws = jnp.where(causal, gm_w_s[i], 0.0).astype(BF16)
        bs = jnp.broadcast_to(gm_b_s[i].astype(F32)[:, :, None], (GM_HEADS, CHUNK, CHUNK))
        y = _mixer(gu, vn, sz, xc, dt, ws, bs, _pad_lanes(a_log[i]), row(jnp.repeat(d_skip[i], SSM_HEAD_DIM)),
                   expand2, batch, seq)
        h = _tail(h1, y, p[i].reshape(batch * seq, -1),
                  _scaled(jnp.concatenate([gm_out_norm[i], ssm_norm[i]]), w_out[i]),
                  _scaled(ffn2_norm[i], ffn2_w_gate[i]), _scaled(ffn2_norm[i], ffn2_w_up[i]),
                  ffn2_w_down[i].astype(BF16), _scaled(ple_norm[i], ple_w_gate[i]), row(ple_b_gate[i]),
                  ple_w_proj[i].astype(BF16), row(final_norm), norm_groups=norm_groups,
                  apply_final_norm=(i == depth - 1))
    return h.reshape(batch, seq, d)
```

```python
import functools
import math

import jax
import jax.numpy as jnp
import numpy as np
from jax import lax
from jax.experimental import pallas as pl
from jax.experimental.pallas import tpu as pltpu

F32 = jnp.float32
BF16 = jnp.bfloat16
EPS = 1e-6

LANES = 128
SUBLANES = 8
VMEM_LIMIT_BYTES = 56 * 1024 * 1024

GM_HEADS = 8
CHUNK = 128
SSM_HEAD_DIM = 64
SSM_GROUPS = 2
SSM_STATE = 128
SSM_CONV = 4

ROW_TILE = 512
WIDE_TILE = 1024
FF_TILE = 256
COL_TILE = 256


def _dot(a, b):
    return jnp.dot(a, b, preferred_element_type=F32)


def _unit_rms(x):
    return x * lax.rsqrt(jnp.mean(x * x, axis=-1, keepdims=True) + EPS)


def _silu(x):
    return x * jax.nn.sigmoid(x)


def _gelu(x):
    return 0.5 * x * (1.0 + lax.erf(x * np.float32(math.sqrt(0.5))))


def _softplus(x):
    return jnp.maximum(x, 0.0) + jnp.log1p(jnp.exp(-jnp.abs(x)))


def _split(x, parts):
    out = []
    for _ in range(parts - 1):
        piece = x.astype(BF16)
        out.append(piece)
        x = x - piece.astype(F32)
    out.append(x.astype(BF16))
    return jnp.concatenate(out, axis=1)


def _anchor(ref, token, never):
    rows, cols = token.shape
    ref[0:rows, 0:cols] = jnp.where(never, token, ref[0:rows, 0:cols])


def _inv_rms(x):
    return lax.rsqrt(jnp.mean(x * x, axis=-1, keepdims=True) + EPS)


def _swiglu(xb_ref, scale, wg_ref, wu_ref, wd_ref, hid_ref, side=None, never=None):
    for j in range(hid_ref.shape[1] // FF_TILE):
        sl = slice(j * FF_TILE, (j + 1) * FF_TILE)
        token = next(side, None) if side is not None and j else None
        if token is not None:
            _anchor(xb_ref, token, never)
        xb = xb_ref[...]
        g = _dot(xb, wg_ref[:, sl])
        u = _dot(xb, wu_ref[:, sl])
        if scale is not None:
            g, u = g * scale, u * scale
        hid_ref[:, sl] = (_silu(g) * u).astype(BF16)
    for _ in side or ():
        pass
    return _dot(hid_ref[...], wd_ref[...])


def _const_spec(shape):
    nd = len(shape)
    return pl.BlockSpec(shape, lambda *_: (0,) * nd, pipeline_mode=pl.Buffered(1))


def _row_spec(cols):
    return pl.BlockSpec((ROW_TILE, cols), lambda i: (i, 0))


def _ffn1_kernel(zero_ref, x_ref, wg_ref, wu_ref, wd_ref, wx_ref, cw_ref, cb_ref, h_ref, n_ref, xc_ref, hid_ref,
                 xn_ref, xe_ref, *, tiles_per_seq):
    i = pl.program_id(0)

    @pl.when(i == 0)
    def _():
        xe_ref[...] = jnp.zeros(xe_ref.shape, F32)

    @pl.when(jnp.logical_and(i > 0, (i - 1) % tiles_per_seq == 0))
    def _():
        xe_ref[0:SUBLANES, :] = jnp.zeros((SUBLANES, xe_ref.shape[1]), F32)

    conv_chunks = [slice(j, j + COL_TILE) for j in range(0, xc_ref.shape[1], COL_TILE)]
    def conv_steps():
        for sl in conv_chunks:
            xe = xe_ref[:, sl]
            p = pltpu.roll(xe, 1, axis=0)
            q = pltpu.roll(cw_ref[1:2, sl] * xe + cw_ref[0:1, sl] * p, 2, axis=0)
            conv = cb_ref[:, sl] + cw_ref[3:4, sl] * xe + cw_ref[2:3, sl] * p + q
            xc_ref[:, sl] = _silu(conv[SUBLANES:, :]).astype(BF16)
            xe_ref[0:SUBLANES, sl] = xe[ROW_TILE:, :]
            yield xc_ref[:, sl]

    x = x_ref[...]
    xn_ref[...] = _unit_rms(x).astype(BF16)
    h1 = x + 0.5 * _swiglu(xn_ref, None, wg_ref, wu_ref, wd_ref, hid_ref, side=conv_steps(),
                           never=zero_ref[0] != 0)
    h_ref[...] = h1
    n = _unit_rms(h1).astype(BF16)
    n_ref[...] = n
    for sl in conv_chunks:
        xe_ref[SUBLANES:, sl] = _dot(n, wx_ref[:, sl])


def _ffn1(x, wg, wu, wd, wx, cw, cb, seq):
    t, d = x.shape
    d_ff = wg.shape[1]
    cd = wx.shape[1]
    tiles = t // ROW_TILE
    cur = lambda cols: pl.BlockSpec((ROW_TILE, cols), lambda i: (jnp.minimum(i, tiles - 1), 0))
    prev = lambda cols: pl.BlockSpec((ROW_TILE, cols), lambda i: (jnp.maximum(i - 1, 0), 0))
    return pl.pallas_call(
        functools.partial(_ffn1_kernel, tiles_per_seq=seq // ROW_TILE),
        grid=(tiles + 1,),
        in_specs=[pl.BlockSpec(memory_space=pltpu.SMEM), cur(d), _const_spec((d, d_ff)), _const_spec((d, d_ff)),
                  _const_spec((d_ff, d)), _const_spec((d, cd)), _const_spec(cw.shape), _const_spec((1, cd))],
        out_specs=(cur(d), cur(d), prev(cd)),
        out_shape=(jax.ShapeDtypeStruct((t, d), F32), jax.ShapeDtypeStruct((t, d), BF16),
                   jax.ShapeDtypeStruct((t, cd), BF16)),
        scratch_shapes=[pltpu.VMEM((ROW_TILE, d_ff), BF16), pltpu.VMEM((ROW_TILE, d), BF16),
                        pltpu.VMEM((SUBLANES + ROW_TILE, cd), F32)],
        compiler_params=pltpu.CompilerParams(dimension_semantics=("arbitrary",),
                                             vmem_limit_bytes=VMEM_LIMIT_BYTES),
        name="ffn1",
    )(jnp.zeros((1,), jnp.int32), x, wg, wu, wd, wx, cw, cb)


def _inproj_kernel(n_ref, wu_ref, wv_ref, wz_ref, wdt_ref, lng_ref, lnb_ref, dtb_ref, ws_ref, bs_ref,
                   ya_ref, sz_ref, dt_ref, vb_ref):
    n = n_ref[...]
    gm = ya_ref.shape[1]
    chunks = lambda width: [slice(j, j + COL_TILE) for j in range(0, width, COL_TILE)]

    row_sum = jnp.zeros((WIDE_TILE, 1), F32)
    for sl in chunks(gm):
        v = _gelu(_dot(n, wv_ref[:, sl]))
        vb_ref[:, sl] = v
        row_sum = row_sum + jnp.sum(v, axis=-1, keepdims=True)
    mean = row_sum * np.float32(1.0 / gm)
    sq_sum = jnp.zeros((WIDE_TILE, 1), F32)
    for sl in chunks(gm):
        sz_ref[:, sl] = _silu(_dot(n, wz_ref[:, sl])).astype(BF16)
        vc = vb_ref[:, sl] - mean
        sq_sum = sq_sum + jnp.sum(vc * vc, axis=-1, keepdims=True)
    inv = lax.rsqrt(sq_sum * np.float32(1.0 / gm) + EPS)
    for sl in chunks(gm):
        gu = _gelu(_dot(n, wu_ref[:, sl]))
        vn = ((vb_ref[:, sl] - mean) * inv * lng_ref[:, sl] + lnb_ref[:, sl]).astype(BF16)
        mixed = []
        for r in range(0, WIDE_TILE, CHUNK):
            heads = [_dot(ws_ref[(sl.start + k) // CHUNK], vn[r:r + CHUNK, k:k + CHUNK]) + bs_ref[(sl.start + k) // CHUNK]
                     for k in range(0, COL_TILE, CHUNK)]
            mixed.append(jnp.concatenate(heads, axis=1))
        ya_ref[:, sl] = (gu * jnp.concatenate(mixed, axis=0)).astype(BF16)
    dt_ref[...] = _softplus(_dot(n, wdt_ref[...]) + dtb_ref[...])


def _inproj(n, wu, wv, wz, wdt, lng, lnb, dtb, ws, bs):
    t, d = n.shape
    gm = wu.shape[1]
    sw = wz.shape[1]
    outs = (jax.ShapeDtypeStruct((t, gm), BF16), jax.ShapeDtypeStruct((t, sw), BF16),
            jax.ShapeDtypeStruct((t, LANES), F32))
    wide = lambda cols: pl.BlockSpec((WIDE_TILE, cols), lambda i: (i, 0))
    return pl.pallas_call(
        _inproj_kernel,
        grid=(t // WIDE_TILE,),
        in_specs=[wide(d), _const_spec((d, gm)), _const_spec((d, gm)), _const_spec((d, sw)),
                  _const_spec((d, LANES)), _const_spec((1, gm)), _const_spec((1, gm)), _const_spec((1, LANES)),
                  _const_spec(ws.shape), _const_spec(bs.shape)],
        out_specs=(wide(gm), wide(sw), wide(LANES)),
        out_shape=outs,
        scratch_shapes=[pltpu.VMEM((WIDE_TILE, gm), F32)],
        compiler_params=pltpu.CompilerParams(dimension_semantics=("parallel",),
                                             vmem_limit_bytes=VMEM_LIMIT_BYTES),
        name="inproj",
    )(n, wu, wv, wz, wdt, lng, lnb, dtb, ws, bs)


def _mixer_kernel(sz_ref, xc_ref, dt_ref, alog_ref, dsk_ref, ex_ref, y_ref, st_ref):
    sw = sz_ref.shape[1]
    grp = sw // SSM_GROUPS
    pairs_per_grp = grp // LANES

    @pl.when(pl.program_id(1) == 0)
    def _():
        st_ref[...] = jnp.zeros(st_ref.shape, F32)

    a2 = -jnp.exp(alog_ref[...]) * np.float32(math.log2(math.e))
    row = lax.broadcasted_iota(jnp.int32, (CHUNK, CHUNK), 0)
    col = lax.broadcasted_iota(jnp.int32, (CHUNK, CHUNK), 1)
    causal = row >= col
    tri = causal.astype(BF16)
    low_half = col < SSM_HEAD_DIM

    for c in range(WIDE_TILE // CHUNK):
        r = slice(c * CHUNK, (c + 1) * CHUNK)


        dt = dt_ref[r, :]
        cum = _dot(tri, _split(dt * a2, 3))
        acs = cum[:, 0:LANES] + cum[:, LANES:2 * LANES] + cum[:, 2 * LANES:3 * LANES]
        out_decay = jnp.exp2(acs)
        dt_state_decay = dt * jnp.exp2(acs[CHUNK - 1:CHUNK, :] - acs)
        src_t = (acs - jnp.log2(dt)).T
        spread = _dot(jnp.concatenate([_split(out_decay, 2), _split(dt_state_decay, 2)], axis=0), ex_ref[...])
        out_decay_x = spread[0:CHUNK, :]
        chunk_decay_x = out_decay_x[CHUNK - 1:CHUNK, :]
        xs_b = xc_ref[r, 0:sw]
        xs = xs_b.astype(F32)
        xw_b = (xs * spread[CHUNK:2 * CHUNK, :]).astype(BF16)
        ys = []
        for g in range(SSM_GROUPS):
            b_g = xc_ref[r, sw + g * SSM_STATE:sw + (g + 1) * SSM_STATE]
            c_g = xc_ref[r, sw + (SSM_GROUPS + g) * SSM_STATE:sw + (SSM_GROUPS + g + 1) * SSM_STATE]
            cb = lax.dot_general(c_g, b_g, (((1,), (1,)), ((), ())), preferred_element_type=F32)
            gs = slice(g * grp, (g + 1) * grp)
            st = st_ref[:, gs]
            y_off = _dot(c_g, st.astype(BF16)) * out_decay_x[:, gs]
            st_ref[:, gs] = st * chunk_decay_x[:, gs] + lax.dot_general(
                b_g, xw_b[:, gs], (((0,), (0,)), ((), ())), preferred_element_type=F32)
            for j in range(pairs_per_grp):
                pair = g * pairs_per_grp + j
                ms = []
                for q in range(2):
                    k = 2 * pair + q
                    diff = acs[:, k:k + 1] - src_t[k:k + 1, :]
                    ms.append((cb * jnp.exp2(jnp.where(causal, diff, -jnp.inf))).astype(BF16))
                slab = xs_b[:, pair * LANES:(pair + 1) * LANES]
                zero = jnp.zeros_like(slab)
                rhs = jnp.concatenate([jnp.where(low_half, slab, zero), jnp.where(low_half, zero, slab)], axis=0)
                y_diag = _dot(jnp.concatenate(ms, axis=1), rhs)
                ys.append(y_diag + y_off[:, j * LANES:(j + 1) * LANES])
        y = jnp.concatenate(ys, axis=1) + xs * dsk_ref[...]
        y_ref[r, :] = (y * sz_ref[r, :].astype(F32)).astype(BF16)


def _mixer(sz, xc, dt, alog, dsk, ex, batch, seq):
    t, sw = sz.shape
    cd = xc.shape[1]
    steps = seq // WIDE_TILE

    def seq_spec(cols):
        return pl.BlockSpec((WIDE_TILE, cols), lambda b, s: (b * steps + s, 0))

    return pl.pallas_call(
        _mixer_kernel,
        grid=(batch, steps),
        in_specs=[seq_spec(sw), seq_spec(cd), seq_spec(LANES),
                  _const_spec((1, LANES)), _const_spec((1, sw)), _const_spec(ex.shape)],
        out_specs=seq_spec(sw),
        out_shape=jax.ShapeDtypeStruct((t, sw), BF16),
        scratch_shapes=[pltpu.VMEM((SSM_STATE, sw), F32)],
        compiler_params=pltpu.CompilerParams(dimension_semantics=("parallel", "arbitrary"),
                                             vmem_limit_bytes=VMEM_LIMIT_BYTES),
        name="mixer",
    )(sz, xc, dt, alog, dsk, ex)


def _tail_kernel(h_ref, ya_ref, yb_ref, p_ref, wo_ref, wg_ref, wu_ref, wd_ref, wpg_ref, bpg_ref, wpp_ref, fn_ref,
                 o_ref, hid_ref, xn_ref, *, norm_groups, apply_final_norm):
    h2 = h_ref[...]
    for src, (lo, hi), (wlo, whi) in norm_groups:
        yg = (ya_ref, yb_ref)[src][:, lo:hi]
        h2 = h2 + _inv_rms(yg.astype(F32)) * _dot(yg, wo_ref[wlo:whi, :])
    xn_ref[...] = h2.astype(BF16)
    h3 = h2 + 0.5 * _swiglu(xn_ref, _inv_rms(h2), wg_ref, wu_ref, wd_ref, hid_ref)
    gate = jax.nn.sigmoid(_dot(h3.astype(BF16), wpg_ref[...]) * _inv_rms(h3) + bpg_ref[...])
    h4 = h3 + gate * _dot(p_ref[...].astype(BF16), wpp_ref[...])
    o_ref[...] = _unit_rms(h4) * fn_ref[...] if apply_final_norm else h4


def _tail(h, ya, yb, p, wo, wg, wu, wd, wpg, bpg, wpp, fn, norm_groups, apply_final_norm):
    t, d = h.shape
    d_ff = wg.shape[1]
    dm = wo.shape[0]
    dp = p.shape[1]
    return pl.pallas_call(
        functools.partial(_tail_kernel, norm_groups=norm_groups, apply_final_norm=apply_final_norm),
        grid=(t // ROW_TILE,),
        in_specs=[_row_spec(d), _row_spec(ya.shape[1]), _row_spec(yb.shape[1]), _row_spec(dp), _const_spec((dm, d)),
                  _const_spec((d, d_ff)), _const_spec((d, d_ff)), _const_spec((d_ff, d)), _const_spec((d, d)),
                  _const_spec((1, d)), _const_spec((dp, d)), _const_spec((1, d))],
        out_specs=_row_spec(d),
        out_shape=jax.ShapeDtypeStruct((t, d), F32),
        scratch_shapes=[pltpu.VMEM((ROW_TILE, d_ff), BF16), pltpu.VMEM((ROW_TILE, d), BF16)],
        compiler_params=pltpu.CompilerParams(dimension_semantics=("parallel",),
                                             vmem_limit_bytes=VMEM_LIMIT_BYTES),
        name="tail",
    )(h, ya, yb, p, wo, wg, wu, wd, wpg, bpg, wpp, fn)


def _pad_lanes(v):
    return jnp.zeros((1, LANES), F32).at[0, :v.shape[0]].set(v.astype(F32))


def _scaled(gain, w):
    return (gain.astype(F32)[:, None] * w.astype(F32)).astype(BF16)


def kernel(x, p, ffn1_norm, ffn1_w_gate, ffn1_w_up, ffn1_w_down, mix_norm, w_in, gm_ln_g, gm_ln_b, gm_w_s, gm_b_s,
           gm_out_norm, conv_w, conv_b, dt_bias, a_log, d_skip, ssm_norm, w_out, ffn2_norm, ffn2_w_gate, ffn2_w_up,
           ffn2_w_down, ple_norm, ple_w_gate, ple_b_gate, ple_w_proj, final_norm):
    batch, seq, d = x.shape
    depth = p.shape[0]
    gm = gm_ln_g.shape[1]
    sw = ssm_norm.shape[1]
    cd = conv_b.shape[1]
    heads = dt_bias.shape[1]
    assert seq % ROW_TILE == 0 and seq % WIDE_TILE == 0 and WIDE_TILE % CHUNK == 0 and heads <= LANES and conv_w.shape[1] == SSM_CONV
    assert gm == sw and gm == GM_HEADS * CHUNK and sw == heads * SSM_HEAD_DIM
    assert cd == sw + 2 * SSM_GROUPS * SSM_STATE

    row = lambda v: v.reshape(1, -1).astype(F32)
    causal = jnp.tril(jnp.ones((CHUNK, CHUNK), dtype=bool))
    expand = (jnp.arange(LANES)[:, None] == jnp.arange(sw)[None, :] // SSM_HEAD_DIM).astype(BF16)
    expand2 = jnp.concatenate([expand, expand], axis=0)
    grp = sw // SSM_GROUPS
    norm_groups = ((0, (0, gm), (0, gm)),) + tuple(
        (1, (g * grp, (g + 1) * grp), (gm + g * grp, gm + (g + 1) * grp)) for g in range(SSM_GROUPS))
    h = x.reshape(batch * seq, d)
    for i in range(depth):
        w = _scaled(mix_norm[i], w_in[i])
        h1, n, xc = _ffn1(h, _scaled(ffn1_norm[i], ffn1_w_gate[i]), _scaled(ffn1_norm[i], ffn1_w_up[i]),
                          ffn1_w_down[i].astype(BF16), w[:, 2 * gm + sw:2 * gm + sw + cd], conv_w[i].astype(F32),
                          row(conv_b[i]), seq)
        w_dt = jnp.zeros((d, LANES), BF16).at[:, :heads].set(w[:, 2 * gm + sw + cd:])
        ws = jnp.where(causal, gm_w_s[i], 0.0).astype(BF16)
        bs = jnp.broadcast_to(gm_b_s[i].astype(F32)[:, :, None], (GM_HEADS, CHUNK, CHUNK))
        ya, sz, dt = _inproj(n, w[:, :gm], w[:, gm:2 * gm], w[:, 2 * gm:2 * gm + sw], w_dt,
                             row(gm_ln_g[i]), row(gm_ln_b[i]), _pad_lanes(dt_bias[i]), ws, bs)
        yb = _mixer(sz, xc, dt, _pad_lanes(a_log[i]), row(jnp.repeat(d_skip[i], SSM_HEAD_DIM)), expand2, batch, seq)
        h = _tail(h1, ya, yb, p[i].reshape(batch * seq, -1),
                  _scaled(jnp.concatenate([gm_out_norm[i], ssm_norm[i]]), w_out[i]),
                  _scaled(ffn2_norm[i], ffn2_w_gate[i]), _scaled(ffn2_norm[i], ffn2_w_up[i]),
                  ffn2_w_down[i].astype(BF16), _scaled(ple_norm[i], ple_w_gate[i]), row(ple_b_gate[i]),
                  ple_w_proj[i].astype(BF16), row(final_norm), norm_groups=norm_groups,
                  apply_final_norm=(i == depth - 1))
    return h.reshape(batch, seq, d)
```

```python
import functools
import math

import jax
import jax.numpy as jnp
import numpy as np
from jax import lax
from jax.experimental import pallas as pl
from jax.experimental.pallas import tpu as pltpu

F32 = jnp.float32
BF16 = jnp.bfloat16
EPS = 1e-6

LANES = 128
SUBLANES = 8
VMEM_LIMIT_BYTES = 56 * 1024 * 1024

GM_HEADS = 8
CHUNK = 128
SSM_HEAD_DIM = 64
SSM_GROUPS = 2
SSM_STATE = 128
SSM_CONV = 4

ROW_TILE = 512
WIDE_TILE = 1024
FF_TILE = 256
COL_TILE = 256


def _dot(a, b):
    return jnp.dot(a, b, preferred_element_type=F32)


def _unit_rms(x):
    return x * lax.rsqrt(jnp.mean(x * x, axis=-1, keepdims=True) + EPS)


def _silu(x):
    return x * jax.nn.sigmoid(x)


def _gelu(x):
    return 0.5 * x * (1.0 + lax.erf(x * np.float32(math.sqrt(0.5))))


def _softplus(x):
    return jnp.maximum(x, 0.0) + jnp.log1p(jnp.exp(-jnp.abs(x)))


def _split(x, parts):
    out = []
    for _ in range(parts - 1):
        piece = x.astype(BF16)
        out.append(piece)
        x = x - piece.astype(F32)
    out.append(x.astype(BF16))
    return jnp.concatenate(out, axis=1)


def _anchor(ref, token, never):
    rows, cols = token.shape
    ref[0:rows, 0:cols] = jnp.where(never, token, ref[0:rows, 0:cols])


def _inv_rms(x):
    return lax.rsqrt(jnp.mean(x * x, axis=-1, keepdims=True) + EPS)


def _swiglu(xb_ref, scale, wg_ref, wu_ref, wd_ref, hid_ref, side=None, never=None):
    for j in range(hid_ref.shape[1] // FF_TILE):
        sl = slice(j * FF_TILE, (j + 1) * FF_TILE)
        token = next(side, None) if side is not None and j else None
        if token is not None:
            _anchor(xb_ref, token, never)
        xb = xb_ref[...]
        g = _dot(xb, wg_ref[:, sl])
        u = _dot(xb, wu_ref[:, sl])
        if scale is not None:
            g, u = g * scale, u * scale
        hid_ref[:, sl] = (_silu(g) * u).astype(BF16)
    for _ in side or ():
        pass
    return _dot(hid_ref[...], wd_ref[...])


def _const_spec(shape):
    nd = len(shape)
    return pl.BlockSpec(shape, lambda *_: (0,) * nd, pipeline_mode=pl.Buffered(1))


def _row_spec(cols):
    return pl.BlockSpec((ROW_TILE, cols), lambda i: (i, 0))


def _ffn1_kernel(zero_ref, x_ref, wg_ref, wu_ref, wd_ref, wx_ref, cw_ref, cb_ref, h_ref, n_ref, xc_ref, hid_ref,
                 xn_ref, xe_ref, *, tiles_per_seq):
    i = pl.program_id(0)

    @pl.when(i == 0)
    def _():
        xe_ref[...] = jnp.zeros(xe_ref.shape, F32)

    @pl.when(jnp.logical_and(i > 0, (i - 1) % tiles_per_seq == 0))
    def _():
        xe_ref[0:SUBLANES, :] = jnp.zeros((SUBLANES, xe_ref.shape[1]), F32)

    conv_chunks = [slice(j, j + COL_TILE) for j in range(0, xc_ref.shape[1], COL_TILE)]
    def conv_steps():
        for sl in conv_chunks:
            xe = xe_ref[:, sl]
            p = pltpu.roll(xe, 1, axis=0)
            q = pltpu.roll(cw_ref[1:2, sl] * xe + cw_ref[0:1, sl] * p, 2, axis=0)
            conv = cb_ref[:, sl] + cw_ref[3:4, sl] * xe + cw_ref[2:3, sl] * p + q
            xc_ref[:, sl] = _silu(conv[SUBLANES:, :]).astype(BF16)
            xe_ref[0:SUBLANES, sl] = xe[ROW_TILE:, :]
            yield xc_ref[:, sl]

    x = x_ref[...]
    xn_ref[...] = _unit_rms(x).astype(BF16)
    h1 = x + 0.5 * _swiglu(xn_ref, None, wg_ref, wu_ref, wd_ref, hid_ref, side=conv_steps(),
                           never=zero_ref[0] != 0)
    h_ref[...] = h1
    n = _unit_rms(h1).astype(BF16)
    n_ref[...] = n
    for sl in conv_chunks:
        xe_ref[SUBLANES:, sl] = _dot(n, wx_ref[:, sl])


def _ffn1(x, wg, wu, wd, wx, cw, cb, seq):
    t, d = x.shape
    d_ff = wg.shape[1]
    cd = wx.shape[1]
    tiles = t // ROW_TILE
    cur = lambda cols: pl.BlockSpec((ROW_TILE, cols), lambda i: (jnp.minimum(i, tiles - 1), 0))
    prev = lambda cols: pl.BlockSpec((ROW_TILE, cols), lambda i: (jnp.maximum(i - 1, 0), 0))
    return pl.pallas_call(
        functools.partial(_ffn1_kernel, tiles_per_seq=seq // ROW_TILE),
        grid=(tiles + 1,),
        in_specs=[pl.BlockSpec(memory_space=pltpu.SMEM), cur(d), _const_spec((d, d_ff)), _const_spec((d, d_ff)),
                  _const_spec((d_ff, d)), _const_spec((d, cd)), _const_spec(cw.shape), _const_spec((1, cd))],
        out_specs=(cur(d), cur(d), prev(cd)),
        out_shape=(jax.ShapeDtypeStruct((t, d), F32), jax.ShapeDtypeStruct((t, d), BF16),
                   jax.ShapeDtypeStruct((t, cd), BF16)),
        scratch_shapes=[pltpu.VMEM((ROW_TILE, d_ff), BF16), pltpu.VMEM((ROW_TILE, d), BF16),
                        pltpu.VMEM((SUBLANES + ROW_TILE, cd), F32)],
        compiler_params=pltpu.CompilerParams(dimension_semantics=("arbitrary",),
                                             vmem_limit_bytes=VMEM_LIMIT_BYTES),
        name="ffn1",
    )(jnp.zeros((1,), jnp.int32), x, wg, wu, wd, wx, cw, cb)


def _inproj_kernel(n_ref, wu_ref, wv_ref, wz_ref, wdt_ref, lng_ref, lnb_ref, dtb_ref, ws_ref, bs_ref,
                   ya_ref, sz_ref, dt_ref, vb_ref):
    n = n_ref[...]
    gm = ya_ref.shape[1]
    chunks = lambda width: [slice(j, j + COL_TILE) for j in range(0, width, COL_TILE)]

    row_sum = jnp.zeros((WIDE_TILE, 1), F32)
    for sl in chunks(gm):
        v = _gelu(_dot(n, wv_ref[:, sl]))
        vb_ref[:, sl] = v
        row_sum = row_sum + jnp.sum(v, axis=-1, keepdims=True)
    mean = row_sum * np.float32(1.0 / gm)
    sq_sum = jnp.zeros((WIDE_TILE, 1), F32)
    for sl in chunks(gm):
        sz_ref[:, sl] = _silu(_dot(n, wz_ref[:, sl])).astype(BF16)
        vc = vb_ref[:, sl] - mean
        sq_sum = sq_sum + jnp.sum(vc * vc, axis=-1, keepdims=True)
    inv = lax.rsqrt(sq_sum * np.float32(1.0 / gm) + EPS)
    for sl in chunks(gm):
        gu = _gelu(_dot(n, wu_ref[:, sl]))
        vn = ((vb_ref[:, sl] - mean) * inv * lng_ref[:, sl] + lnb_ref[:, sl]).astype(BF16)
        mixed = []
        for r in range(0, WIDE_TILE, CHUNK):
            heads = [_dot(ws_ref[(sl.start + k) // CHUNK], vn[r:r + CHUNK, k:k + CHUNK]) + bs_ref[(sl.start + k) // CHUNK]
                     for k in range(0, COL_TILE, CHUNK)]
            mixed.append(jnp.concatenate(heads, axis=1))
        ya_ref[:, sl] = (gu * jnp.concatenate(mixed, axis=0)).astype(BF16)
    dt_ref[...] = _softplus(_dot(n, wdt_ref[...]) + dtb_ref[...])


def _inproj(n, wu, wv, wz, wdt, lng, lnb, dtb, ws, bs):
    t, d = n.shape
    gm = wu.shape[1]
    sw = wz.shape[1]
    outs = (jax.ShapeDtypeStruct((t, gm), BF16), jax.ShapeDtypeStruct((t, sw), BF16),
            jax.ShapeDtypeStruct((t, LANES), F32))
    wide = lambda cols: pl.BlockSpec((WIDE_TILE, cols), lambda i: (i, 0))
    return pl.pallas_call(
        _inproj_kernel,
        grid=(t // WIDE_TILE,),
        in_specs=[wide(d), _const_spec((d, gm)), _const_spec((d, gm)), _const_spec((d, sw)),
                  _const_spec((d, LANES)), _const_spec((1, gm)), _const_spec((1, gm)), _const_spec((1, LANES)),
                  _const_spec(ws.shape), _const_spec(bs.shape)],
        out_specs=(wide(gm), wide(sw), wide(LANES)),
        out_shape=outs,
        scratch_shapes=[pltpu.VMEM((WIDE_TILE, gm), F32)],
        compiler_params=pltpu.CompilerParams(dimension_semantics=("parallel",),
                                             vmem_limit_bytes=VMEM_LIMIT_BYTES),
        name="inproj",
    )(n, wu, wv, wz, wdt, lng, lnb, dtb, ws, bs)


def _mixer_kernel(xc_ref, dt_ref, alog_ref, dsk_ref, ex_ref, y_ref, st_ref):
    sw = y_ref.shape[1]
    grp = sw // SSM_GROUPS
    pairs_per_grp = grp // LANES

    @pl.when(pl.program_id(1) == 0)
    def _():
        st_ref[...] = jnp.zeros(st_ref.shape, F32)

    a2 = -jnp.exp(alog_ref[...]) * np.float32(math.log2(math.e))
    row = lax.broadcasted_iota(jnp.int32, (CHUNK, CHUNK), 0)
    col = lax.broadcasted_iota(jnp.int32, (CHUNK, CHUNK), 1)
    causal = row >= col
    tri = causal.astype(BF16)
    low_half = col < SSM_HEAD_DIM

    for c in range(WIDE_TILE // CHUNK):
        r = slice(c * CHUNK, (c + 1) * CHUNK)


        dt = dt_ref[r, :]
        cum = _dot(tri, _split(dt * a2, 3))
        acs = cum[:, 0:LANES] + cum[:, LANES:2 * LANES] + cum[:, 2 * LANES:3 * LANES]
        out_decay = jnp.exp2(acs)
        dt_state_decay = dt * jnp.exp2(acs[CHUNK - 1:CHUNK, :] - acs)
        src_t = (acs - jnp.log2(dt)).T
        spread = _dot(jnp.concatenate([_split(out_decay, 2), _split(dt_state_decay, 2)], axis=0), ex_ref[...])
        out_decay_x = spread[0:CHUNK, :]
        chunk_decay_x = out_decay_x[CHUNK - 1:CHUNK, :]
        xs_b = xc_ref[r, 0:sw]
        xs = xs_b.astype(F32)
        xw_b = (xs * spread[CHUNK:2 * CHUNK, :]).astype(BF16)
        ys = []
        for g in range(SSM_GROUPS):
            b_g = xc_ref[r, sw + g * SSM_STATE:sw + (g + 1) * SSM_STATE]
            c_g = xc_ref[r, sw + (SSM_GROUPS + g) * SSM_STATE:sw + (SSM_GROUPS + g + 1) * SSM_STATE]
            cb = lax.dot_general(c_g, b_g, (((1,), (1,)), ((), ())), preferred_element_type=F32)
            gs = slice(g * grp, (g + 1) * grp)
            st = st_ref[:, gs]
            y_off = _dot(c_g, st.astype(BF16)) * out_decay_x[:, gs]
            st_ref[:, gs] = st * chunk_decay_x[:, gs] + lax.dot_general(
                b_g, xw_b[:, gs], (((0,), (0,)), ((), ())), preferred_element_type=F32)
            for j in range(pairs_per_grp):
                pair = g * pairs_per_grp + j
                ms = []
                for q in range(2):
                    k = 2 * pair + q
                    diff = acs[:, k:k + 1] - src_t[k:k + 1, :]
                    ms.append((cb * jnp.exp2(jnp.where(causal, diff, -jnp.inf))).astype(BF16))
                slab = xs_b[:, pair * LANES:(pair + 1) * LANES]
                zero = jnp.zeros_like(slab)
                rhs = jnp.concatenate([jnp.where(low_half, slab, zero), jnp.where(low_half, zero, slab)], axis=0)
                y_diag = _dot(jnp.concatenate(ms, axis=1), rhs)
                ys.append(y_diag + y_off[:, j * LANES:(j + 1) * LANES])
        y = jnp.concatenate(ys, axis=1) + xs * dsk_ref[...]
        y_ref[r, :] = y.astype(BF16)


def _mixer(xc, dt, alog, dsk, ex, batch, seq):
    t, cd = xc.shape
    sw = dsk.shape[1]
    steps = seq // WIDE_TILE

    def seq_spec(cols):
        return pl.BlockSpec((WIDE_TILE, cols), lambda b, s: (b * steps + s, 0))

    return pl.pallas_call(
        _mixer_kernel,
        grid=(batch, steps),
        in_specs=[seq_spec(cd), seq_spec(LANES),
                  _const_spec((1, LANES)), _const_spec((1, sw)), _const_spec(ex.shape)],
        out_specs=seq_spec(sw),
        out_shape=jax.ShapeDtypeStruct((t, sw), BF16),
        scratch_shapes=[pltpu.VMEM((SSM_STATE, sw), F32)],
        compiler_params=pltpu.CompilerParams(dimension_semantics=("parallel", "arbitrary"),
                                             vmem_limit_bytes=VMEM_LIMIT_BYTES),
        name="mixer",
    )(xc, dt, alog, dsk, ex)


def _tail_kernel(h_ref, ya_ref, yb_ref, sz_ref, p_ref, wo_ref, wg_ref, wu_ref, wd_ref, wpg_ref, bpg_ref, wpp_ref,
                 fn_ref, o_ref, hid_ref, xn_ref, *, norm_groups, apply_final_norm):
    h2 = h_ref[...]
    for src, (lo, hi), (wlo, whi) in norm_groups:
        if src:
            yf = yb_ref[:, lo:hi].astype(F32) * sz_ref[:, lo:hi].astype(F32)
            yg = yf.astype(BF16)
        else:
            yg = ya_ref[:, lo:hi]
            yf = yg.astype(F32)
        h2 = h2 + _inv_rms(yf) * _dot(yg, wo_ref[wlo:whi, :])
    xn_ref[...] = h2.astype(BF16)
    h3 = h2 + 0.5 * _swiglu(xn_ref, _inv_rms(h2), wg_ref, wu_ref, wd_ref, hid_ref)
    gate = jax.nn.sigmoid(_dot(h3.astype(BF16), wpg_ref[...]) * _inv_rms(h3) + bpg_ref[...])
    h4 = h3 + gate * _dot(p_ref[...].astype(BF16), wpp_ref[...])
    o_ref[...] = _unit_rms(h4) * fn_ref[...] if apply_final_norm else h4


def _tail(h, ya, yb, sz, p, wo, wg, wu, wd, wpg, bpg, wpp, fn, norm_groups, apply_final_norm):
    t, d = h.shape
    d_ff = wg.shape[1]
    dm = wo.shape[0]
    dp = p.shape[1]
    return pl.pallas_call(
        functools.partial(_tail_kernel, norm_groups=norm_groups, apply_final_norm=apply_final_norm),
        grid=(t // ROW_TILE,),
        in_specs=[_row_spec(d), _row_spec(ya.shape[1]), _row_spec(yb.shape[1]), _row_spec(sz.shape[1]),
                  _row_spec(dp), _const_spec((dm, d)),
                  _const_spec((d, d_ff)), _const_spec((d, d_ff)), _const_spec((d_ff, d)), _const_spec((d, d)),
                  _const_spec((1, d)), _const_spec((dp, d)), _const_spec((1, d))],
        out_specs=_row_spec(d),
        out_shape=jax.ShapeDtypeStruct((t, d), F32),
        scratch_shapes=[pltpu.VMEM((ROW_TILE, d_ff), BF16), pltpu.VMEM((ROW_TILE, d), BF16)],
        compiler_params=pltpu.CompilerParams(dimension_semantics=("parallel",),
                                             vmem_limit_bytes=VMEM_LIMIT_BYTES),
        name="tail",
    )(h, ya, yb, sz, p, wo, wg, wu, wd, wpg, bpg, wpp, fn)


def _pad_lanes(v):
    return jnp.zeros((1, LANES), F32).at[0, :v.shape[0]].set(v.astype(F32))


def _scaled(gain, w):
    return (gain.astype(F32)[:, None] * w.astype(F32)).astype(BF16)


def kernel(x, p, ffn1_norm, ffn1_w_gate, ffn1_w_up, ffn1_w_down, mix_norm, w_in, gm_ln_g, gm_ln_b, gm_w_s, gm_b_s,
           gm_out_norm, conv_w, conv_b, dt_bias, a_log, d_skip, ssm_norm, w_out, ffn2_norm, ffn2_w_gate, ffn2_w_up,
           ffn2_w_down, ple_norm, ple_w_gate, ple_b_gate, ple_w_proj, final_norm):
    batch, seq, d = x.shape
    depth = p.shape[0]
    gm = gm_ln_g.shape[1]
    sw = ssm_norm.shape[1]
    cd = conv_b.shape[1]
    heads = dt_bias.shape[1]
    assert seq % ROW_TILE == 0 and seq % WIDE_TILE == 0 and WIDE_TILE % CHUNK == 0 and heads <= LANES and conv_w.shape[1] == SSM_CONV
    assert gm == sw and gm == GM_HEADS * CHUNK and sw == heads * SSM_HEAD_DIM
    assert cd == sw + 2 * SSM_GROUPS * SSM_STATE

    row = lambda v: v.reshape(1, -1).astype(F32)
    causal = jnp.tril(jnp.ones((CHUNK, CHUNK), dtype=bool))
    expand = (jnp.arange(LANES)[:, None] == jnp.arange(sw)[None, :] // SSM_HEAD_DIM).astype(BF16)
    expand2 = jnp.concatenate([expand, expand], axis=0)
    grp = sw // SSM_GROUPS
    norm_groups = ((0, (0, gm), (0, gm)),) + tuple(
        (1, (g * grp, (g + 1) * grp), (gm + g * grp, gm + (g + 1) * grp)) for g in range(SSM_GROUPS))
    h = x.reshape(batch * seq, d)
    for i in range(depth):
        w = _scaled(mix_norm[i], w_in[i])
        h1, n, xc = _ffn1(h, _scaled(ffn1_norm[i], ffn1_w_gate[i]), _scaled(ffn1_norm[i], ffn1_w_up[i]),
                          ffn1_w_down[i].astype(BF16), w[:, 2 * gm + sw:2 * gm + sw + cd], conv_w[i].astype(F32),
                          row(conv_b[i]), seq)
        w_dt = jnp.zeros((d, LANES), BF16).at[:, :heads].set(w[:, 2 * gm + sw + cd:])
        ws = jnp.where(causal, gm_w_s[i], 0.0).astype(BF16)
        bs = jnp.broadcast_to(gm_b_s[i].astype(F32)[:, :, None], (GM_HEADS, CHUNK, CHUNK))
        ya, sz, dt = _inproj(n, w[:, :gm], w[:, gm:2 * gm], w[:, 2 * gm:2 * gm + sw], w_dt,
                             row(gm_ln_g[i]), row(gm_ln_b[i]), _pad_lanes(dt_bias[i]), ws, bs)
        yb = _mixer(xc, dt, _pad_lanes(a_log[i]), row(jnp.repeat(d_skip[i], SSM_HEAD_DIM)), expand2, batch, seq)
        h = _tail(h1, ya, yb, sz, p[i].reshape(batch * seq, -1),
                  _scaled(jnp.concatenate([gm_out_norm[i], ssm_norm[i]]), w_out[i]),
                  _scaled(ffn2_norm[i], ffn2_w_gate[i]), _scaled(ffn2_norm[i], ffn2_w_up[i]),
                  ffn2_w_down[i].astype(BF16), _scaled(ple_norm[i], ple_w_gate[i]), row(ple_b_gate[i]),
                  ple_w_proj[i].astype(BF16), row(final_norm), norm_groups=norm_groups,
                  apply_final_norm=(i == depth - 1))
    return h.reshape(batch, seq, d)
```
